```python
import math, functools
import jax, jax.numpy as jnp
from jax import lax
import numpy as np

D_MODEL = 1024
BATCH = 1
SEQ = 16384
DEPTH = 2
DEC_BATCH = 128
DEC_SEQ = 1
PAST_LEN = 16384
PAGE_SIZE = 128

ROPE_THETA = 10000.0
NORM_EPS = 1e-6
LN_EPS = 1e-5
NEG_INF = -1e30
QBLOCK = 128

MLA_HEADS = 8
MLA_NOPE = 64
MLA_ROPE = 32
MLA_V = 64
MLA_Q_LORA = 256
MLA_KV_LORA = 128
MLA_SCALE = (MLA_NOPE + MLA_ROPE) ** -0.5
DIFF_HEADS = 4
DIFF_KV_HEADS = 2
DIFF_REP = DIFF_HEADS // DIFF_KV_HEADS
DIFF_DH = 32
DIFF_V = 2 * DIFF_DH
DIFF_SCALE = DIFF_DH ** -0.5
CHUNK = 128
CHUNK_GROUPS = 4
CHUNK_W = 256
CHUNK_GW = CHUNK_W // CHUNK_GROUPS
MLA_OUT = MLA_HEADS * MLA_V
DIFF_OUT = DIFF_HEADS * DIFF_V
MIX_W = MLA_OUT + DIFF_OUT + CHUNK_W
IN_SIZES = (MLA_Q_LORA, MLA_KV_LORA, MLA_ROPE,
            DIFF_HEADS * 2 * DIFF_DH, DIFF_KV_HEADS * 2 * DIFF_DH, DIFF_KV_HEADS * DIFF_V,
            CHUNK_W, CHUNK_W)
PROJ_IN = sum(IN_SIZES)
IN_SPLITS = tuple(int(c) for c in np.cumsum(IN_SIZES)[:-1])
D_FF = 2816
N_EXPERTS = 8
TOP_K = 2
D_FF_EXPERT = 3584

kernel_name = 'hybrid_mla_diffattn_chunkmlp_decoder_step'


def rmsnorm(x, g, eps=NORM_EPS):
    xf = x.astype(jnp.float32)
    y = xf * lax.rsqrt(jnp.mean(xf * xf, axis=-1, keepdims=True) + eps)
    return (y * g.astype(jnp.float32)).astype(x.dtype)


def layernorm(x, g, b, eps=LN_EPS):
    xf = x.astype(jnp.float32)
    mu = jnp.mean(xf, axis=-1, keepdims=True)
    xc = xf - mu
    var = jnp.mean(xc * xc, axis=-1, keepdims=True)
    return (xc * lax.rsqrt(var + eps) * g.astype(jnp.float32) + b.astype(jnp.float32)).astype(x.dtype)


def rope(x, pos):
    half = x.shape[-1] // 2
    inv = 1.0 / (ROPE_THETA ** (jnp.arange(half, dtype=jnp.float32) / half))
    ang = pos.astype(jnp.float32)[:, None] * inv[None, :]
    ang = ang.reshape((1, pos.shape[0]) + (1,) * (x.ndim - 3) + (half,))
    cos, sin = jnp.cos(ang), jnp.sin(ang)
    xf = x.astype(jnp.float32)
    x1, x2 = xf[..., :half], xf[..., half:]
    return jnp.concatenate([x1 * cos - x2 * sin, x2 * cos + x1 * sin], axis=-1).astype(x.dtype)


def attend(q, keysets, score_fn, combine_fn, value_fn):
    scores = []
    for k, v, mask in keysets:
        s = score_fn(q, k).astype(jnp.float32)
        scores.append(s if mask is None else jnp.where(mask, s, NEG_INF))
    w = combine_fn(jax.nn.softmax(jnp.concatenate(scores, axis=-1), axis=-1))
    if len(keysets) == 1:
        parts = [w]
    else:
        offs = tuple(int(c) for c in np.cumsum([s.shape[-1] for s in scores])[:-1])
        parts = jnp.split(w, offs, axis=-1)
    out = None
    for wi, (k, v, _) in zip(parts, keysets):
        o = value_fn(wi, k, v)
        out = o if out is None else out + o
    return out


def mla_scores(q, k):
    q_lat, q_rope = q
    c, kr = k
    return (jnp.einsum('bqhc,bkc->bhqk', q_lat, c) + jnp.einsum('bqhr,bkr->bhqk', q_rope, kr)) * MLA_SCALE


def mla_values(w, k, v):
    c = k[0]
    return jnp.einsum('bhqk,bkc->bqhc', w.astype(c.dtype), c)


def identity_combine(p):
    return p


def diff_scores(q, k):
    return jnp.einsum('bqgrmd,bkgmd->mbgrqk', q, k) * DIFF_SCALE


def diff_combine(p, lam):
    return p[0] - lam * p[1]


def diff_values(w, k, v):
    return jnp.einsum('bgrqk,bkgv->bqgrv', w.astype(v.dtype), v)


def prompt_attention(q, k, v, score_fn, combine_fn, value_fn):
    B, S = jax.tree_util.tree_leaves(q)[0].shape[:2]
    nb = S // QBLOCK
    to_blocks = lambda a: jnp.moveaxis(a.reshape((B, nb, QBLOCK) + a.shape[2:]), 1, 0)
    qb = jax.tree_util.tree_map(to_blocks, q)
    kpos = jnp.arange(S)

    def one(inp):
        i, qi = inp
        qpos = i * QBLOCK + jnp.arange(QBLOCK)
        mask = kpos[None, :] <= qpos[:, None]
        return attend(qi, [(k, v, mask)], score_fn, combine_fn, value_fn)

    o = jnp.moveaxis(lax.map(one, (jnp.arange(nb), qb)), 0, 1)
    return o.reshape((B, S) + o.shape[3:])


def paged_rows(cache, layer, pages):
    g = cache[layer, pages]
    return g.reshape((-1,) + g.shape[2:])


def sample_attention(q, k_new, v_new, k_cache, v_cache, layer, page_table, score_fn, combine_fn, value_fn):
    T = jax.tree_util.tree_leaves(k_new)[0].shape[1]
    mask_new = jnp.tril(jnp.ones((T, T), dtype=bool))
    add_b = lambda t: jax.tree_util.tree_map(lambda a: a[None], t)
    gather = lambda cs, pages: jax.tree_util.tree_map(lambda c: paged_rows(c, layer, pages)[None], cs)

    def one(inp):
        pages, qs, kn, vn = inp
        keysets = [(gather(k_cache, pages), gather(v_cache, pages), None),
                   (add_b(kn), add_b(vn), mask_new)]
        return attend(add_b(qs), keysets, score_fn, combine_fn, value_fn)[0]

    return lax.map(one, (page_table, q, k_new, v_new))


def chunk_mix(v, w_s, b_s):
    B, T, _ = v.shape
    nc = -(-T // CHUNK)
    vp = jnp.pad(v, ((0, 0), (0, nc * CHUNK - T), (0, 0))).reshape(B, nc, CHUNK, CHUNK_GROUPS, CHUNK_GW)
    tri = jnp.tril(jnp.ones((CHUNK, CHUNK), dtype=bool))
    w = jnp.where(tri[None], w_s, jnp.zeros_like(w_s)).astype(v.dtype)
    out = jnp.einsum('gts,bcsgd->bctgd', w, vp) + jnp.transpose(b_s).astype(v.dtype)[None, None, :, :, None]
    return out.reshape(B, nc * CHUNK, CHUNK_W)[:, :T]


def swiglu(x, wg, wu, wd):
    return (jax.nn.silu(x @ wg) * (x @ wu)) @ wd


def moe_ffn(x, w_router, w_gate, w_up, w_down):
    logits = (x @ w_router).astype(jnp.float32)
    top_v, top_i = lax.top_k(logits, TOP_K)
    gates = jax.nn.softmax(top_v, axis=-1)
    dense_gate = jnp.sum(jax.nn.one_hot(top_i, N_EXPERTS, dtype=jnp.float32) * gates[..., None], axis=-2)
    out = None
    for e in range(N_EXPERTS):
        o = dense_gate[..., e:e + 1].astype(x.dtype) * swiglu(x, w_gate[e], w_up[e], w_down[e])
        out = o if out is None else out + o
    return out


def token_mixer(xn, pos, l, P, mla_attn, diff_attn):
    B, T, _ = xn.shape
    z = xn @ P['w_in'][l]
    cq, ckv, kr, dq, dk, dv, cu, cv = jnp.split(z, IN_SPLITS, axis=-1)
    q = jnp.einsum('btc,chd->bthd', rmsnorm(cq, P['mla_q_norm'][l]), P['mla_w_uq'][l])
    q_nope, q_rope = q[..., :MLA_NOPE], rope(q[..., MLA_NOPE:], pos)
    q_lat = jnp.einsum('bthn,chn->bthc', q_nope, P['mla_w_uk'][l])
    ckv = rmsnorm(ckv, P['mla_kv_norm'][l])
    kr = rope(kr, pos)
    o_lat = mla_attn((q_lat, q_rope), (ckv, kr))
    o_mla = jnp.einsum('bthc,chv->bthv', o_lat, P['mla_w_uv'][l]).reshape(B, T, MLA_OUT)
    dq = rope(dq.reshape(B, T, DIFF_KV_HEADS, DIFF_REP, 2, DIFF_DH), pos)
    dk = rope(dk.reshape(B, T, DIFF_KV_HEADS, 2, DIFF_DH), pos)
    dv = dv.reshape(B, T, DIFF_KV_HEADS, DIFF_V)
    lam_init = 0.8 - 0.6 * math.exp(-0.3 * l)
    f32 = lambda a: a.astype(jnp.float32)
    lam = (jnp.exp(jnp.sum(f32(P['diff_lambda_q1'][l]) * f32(P['diff_lambda_k1'][l])))
           - jnp.exp(jnp.sum(f32(P['diff_lambda_q2'][l]) * f32(P['diff_lambda_k2'][l]))) + lam_init)
    o = diff_attn(dq, dk, dv, lam)
    o_diff = (rmsnorm(o, P['diff_subln'][l], LN_EPS) * (1.0 - lam_init)).reshape(B, T, DIFF_OUT)
    u = jax.nn.gelu(cu)
    v = layernorm(jax.nn.gelu(cv), P['chunk_ln_g'][l], P['chunk_ln_b'][l])
    o_chunk = u * chunk_mix(v, P['chunk_w_s'][l], P['chunk_b_s'][l])
    out = jnp.concatenate([o_mla, o_diff, o_chunk], axis=-1) @ P['w_out'][l]
    return out, (ckv, kr, dk, dv, v)


def trunk(x, pos, P, make_attn):
    states = []
    for l in range(DEPTH):
        mla_attn, diff_attn = make_attn(l)
        out, st = token_mixer(rmsnorm(x, P['norm_mix'][l]), pos, l, P, mla_attn, diff_attn)
        x = x + out
        h = rmsnorm(x, P['norm_ffn'][l])
        j = l // 2
        if l % 2 == 0:
            f = swiglu(h, P['ffn_w_gate'][j], P['ffn_w_up'][j], P['ffn_w_down'][j])
        else:
            f = moe_ffn(h, P['moe_w_router'][j], P['moe_w_gate'][j], P['moe_w_up'][j], P['moe_w_down'][j])
        x = x + f
        states.append(st)
    return rmsnorm(x, P['norm_final']), states


def setup_inputs(seed: int = 0) -> dict:
    key = jax.random.key(seed)
    ks = jax.random.split(key, 40)
    nrm = lambda i, shape, scale: jax.random.normal(ks[i], shape, jnp.float32) * scale
    gain = lambda i, shape: 1.0 + 0.02 * jax.random.normal(ks[i], shape, jnp.float32)
    n_pages = PAST_LEN // PAGE_SIZE
    n_pool = (5 * DEC_BATCH * n_pages + 3) // 4
    perm = jax.random.permutation(ks[6], n_pool)[:DEC_BATCH * n_pages]
    page_table = perm.reshape(DEC_BATCH, n_pages).astype(jnp.int32)
    n_dense = (DEPTH + 1) // 2
    n_moe = DEPTH // 2
    return {
        'x_prompt': nrm(0, (BATCH, SEQ, D_MODEL), 1.0),
        'x_sample': nrm(1, (DEC_BATCH, DEC_SEQ, D_MODEL), 1.0),
        'cache_mla_latent': nrm(2, (DEPTH, n_pool, PAGE_SIZE, MLA_KV_LORA), 1.0),
        'cache_mla_krope': nrm(3, (DEPTH, n_pool, PAGE_SIZE, MLA_ROPE), 1.0),
        'cache_diff_k': nrm(4, (DEPTH, n_pool, PAGE_SIZE, DIFF_KV_HEADS, 2, DIFF_DH), 1.0),
        'cache_diff_v': nrm(5, (DEPTH, n_pool, PAGE_SIZE, DIFF_KV_HEADS, DIFF_V), 1.0),
        'page_table': page_table,
        'norm_mix': gain(7, (DEPTH, D_MODEL)),
        'w_in': nrm(8, (DEPTH, D_MODEL, PROJ_IN), D_MODEL ** -0.5),
        'mla_q_norm': gain(9, (DEPTH, MLA_Q_LORA)),
        'mla_w_uq': nrm(10, (DEPTH, MLA_Q_LORA, MLA_HEADS, MLA_NOPE + MLA_ROPE), MLA_Q_LORA ** -0.5),
        'mla_kv_norm': gain(11, (DEPTH, MLA_KV_LORA)),
        'mla_w_uk': nrm(12, (DEPTH, MLA_KV_LORA, MLA_HEADS, MLA_NOPE), MLA_KV_LORA ** -0.5),
        'mla_w_uv': nrm(13, (DEPTH, MLA_KV_LORA, MLA_HEADS, MLA_V), MLA_KV_LORA ** -0.5),
        'diff_lambda_q1': nrm(14, (DEPTH, DIFF_DH), 0.1),
        'diff_lambda_k1': nrm(15, (DEPTH, DIFF_DH), 0.1),
        'diff_lambda_q2': nrm(16, (DEPTH, DIFF_DH), 0.1),
        'diff_lambda_k2': nrm(17, (DEPTH, DIFF_DH), 0.1),
        'diff_subln': gain(18, (DEPTH, DIFF_V)),
        'chunk_ln_g': gain(19, (DEPTH, CHUNK_W)),
        'chunk_ln_b': nrm(20, (DEPTH, CHUNK_W), 0.02),
        'chunk_w_s': nrm(21, (DEPTH, CHUNK_GROUPS, CHUNK, CHUNK), CHUNK ** -0.5),
        'chunk_b_s': gain(22, (DEPTH, CHUNK_GROUPS, CHUNK)),
        'w_out': nrm(23, (DEPTH, MIX_W, D_MODEL), MIX_W ** -0.5),
        'norm_ffn': gain(24, (DEPTH, D_MODEL)),
        'ffn_w_gate': nrm(25, (n_dense, D_MODEL, D_FF), D_MODEL ** -0.5),
        'ffn_w_up': nrm(26, (n_dense, D_MODEL, D_FF), D_MODEL ** -0.5),
        'ffn_w_down': nrm(27, (n_dense, D_FF, D_MODEL), D_FF ** -0.5),
        'moe_w_router': nrm(28, (n_moe, D_MODEL, N_EXPERTS), D_MODEL ** -0.5),
        'moe_w_gate': nrm(29, (n_moe, N_EXPERTS, D_MODEL, D_FF_EXPERT), D_MODEL ** -0.5),
        'moe_w_up': nrm(30, (n_moe, N_EXPERTS, D_MODEL, D_FF_EXPERT), D_MODEL ** -0.5),
        'moe_w_down': nrm(31, (n_moe, N_EXPERTS, D_FF_EXPERT, D_MODEL), D_FF_EXPERT ** -0.5),
        'norm_final': gain(32, (D_MODEL,)),
    }


def reference(x_prompt, x_sample, cache_mla_latent, cache_mla_krope, cache_diff_k, cache_diff_v, page_table,
              norm_mix, w_in, mla_q_norm, mla_w_uq, mla_kv_norm, mla_w_uk, mla_w_uv,
              diff_lambda_q1, diff_lambda_k1, diff_lambda_q2, diff_lambda_k2, diff_subln,
              chunk_ln_g, chunk_ln_b, chunk_w_s, chunk_b_s, w_out, norm_ffn,
              ffn_w_gate, ffn_w_up, ffn_w_down, moe_w_router, moe_w_gate, moe_w_up, moe_w_down, norm_final):
    P = dict(norm_mix=norm_mix, w_in=w_in, mla_q_norm=mla_q_norm, mla_w_uq=mla_w_uq, mla_kv_norm=mla_kv_norm,
             mla_w_uk=mla_w_uk, mla_w_uv=mla_w_uv, diff_lambda_q1=diff_lambda_q1, diff_lambda_k1=diff_lambda_k1,
             diff_lambda_q2=diff_lambda_q2, diff_lambda_k2=diff_lambda_k2, diff_subln=diff_subln,
             chunk_ln_g=chunk_ln_g, chunk_ln_b=chunk_ln_b, chunk_w_s=chunk_w_s, chunk_b_s=chunk_b_s,
             w_out=w_out, norm_ffn=norm_ffn, ffn_w_gate=ffn_w_gate, ffn_w_up=ffn_w_up, ffn_w_down=ffn_w_down,
             moe_w_router=moe_w_router, moe_w_gate=moe_w_gate, moe_w_up=moe_w_up, moe_w_down=moe_w_down,
             norm_final=norm_final)

    def prompt_fns(l):
        mla = lambda q, k: prompt_attention(q, k, None, mla_scores, identity_combine, mla_values)
        diff = lambda q, k, v, lam: prompt_attention(q, k, v, diff_scores,
                                                     functools.partial(diff_combine, lam=lam), diff_values)
        return mla, diff

    def sample_fns(l):
        mla = lambda q, k: sample_attention(q, k, None, (cache_mla_latent, cache_mla_krope), None, l, page_table,
                                            mla_scores, identity_combine, mla_values)
        diff = lambda q, k, v, lam: sample_attention(q, k, v, cache_diff_k, cache_diff_v, l, page_table, diff_scores,
                                                     functools.partial(diff_combine, lam=lam), diff_values)
        return mla, diff

    pos_p = jnp.arange(x_prompt.shape[1])
    past_len = page_table.shape[1] * PAGE_SIZE
    pos_s = past_len + jnp.arange(x_sample.shape[1])
    y_prompt, st_p = trunk(x_prompt, pos_p, P, prompt_fns)
    y_sample, st_s = trunk(x_sample, pos_s, P, sample_fns)
    stack = lambda sts, i: jnp.stack([s[i] for s in sts], axis=0)
    return (y_prompt, y_sample,
            stack(st_p, 0), stack(st_p, 1), stack(st_p, 2), stack(st_p, 3),
            stack(st_s, 0), stack(st_s, 1), stack(st_s, 2), stack(st_s, 3), stack(st_s, 4))
```

```python
import functools
import math

import jax
import jax.numpy as jnp
from jax import lax
from jax.experimental import pallas as pl
from jax.experimental.pallas import tpu as pltpu

F32 = jnp.float32
BF16 = jnp.bfloat16

ROPE_THETA = 10000.0
NORM_EPS = 1e-6
LN_EPS = 1e-5
NEG_INF = -1e30
TOP_K = 2

LANES = 128
HEADS = 8
ROPE_HALF = 16
CHUNK = 128
VMEM_LIMIT = 56 * 1024 * 1024


def _params(sem):
    return pltpu.CompilerParams(dimension_semantics=sem, vmem_limit_bytes=VMEM_LIMIT)


def _rms(x, g, eps):
    return x * lax.rsqrt(jnp.mean(x * x, axis=-1, keepdims=True) + eps) * g


def _gelu(x):
    return 0.5 * x * (1.0 + jnp.tanh(math.sqrt(2.0 / math.pi) * (x + 0.044715 * (x * x * x))))


def _full(shape):
    return pl.BlockSpec(shape, lambda *_: (0,) * len(shape))


def _fold_q_kernel(uq_ref, uk_ref, o_ref):
    for h in range(HEADS):
        o_ref[:, h * LANES:(h + 1) * LANES] = lax.dot_general(
            uq_ref[h], uk_ref[h], (((1,), (1,)), ((), ())),
            precision=lax.Precision.HIGHEST, preferred_element_type=F32)


def _fold_q(uq_nope, uk):
    h, cq, _ = uq_nope.shape
    return pl.pallas_call(
        _fold_q_kernel,
        out_shape=jax.ShapeDtypeStruct((cq, h * uk.shape[1]), F32),
        name="fold_q",
    )(uq_nope, uk)


def _prep_kernel(x_ref, gmix_ref, win_ref, gq_ref, wq_ref, gkv_ref, cos_ref, sin_ref, lng_ref, lnb_ref,
                 ws_ref, bs_ref,
                 qm_ref, kc_ref, qd_ref, kd_ref, vd_ref, oc_ref, ckv_ref, kr_ref, dk_ref, dv_ref, *rest,
                 sample_mode, mla_scale, diff_scale):
    tm = x_ref.shape[0]
    x = x_ref[...]
    xn = _rms(x, gmix_ref[...], NORM_EPS).astype(BF16)
    z = jnp.dot(xn, win_ref[...], preferred_element_type=F32)
    cos = cos_ref[...]
    sin = sin_ref[...]
    col = lambda a, n=1: z[:, a * LANES:(a + n) * LANES]

    cqn = _rms(col(0, 2), gq_ref[...], NORM_EPS).astype(BF16)
    qall = jnp.dot(cqn, wq_ref[...], preferred_element_type=F32)
    for h in range(HEADS):
        qlat = qall[:, h * LANES:(h + 1) * LANES]
        qa = qall[:, (HEADS + h) * LANES:(HEADS + h + 1) * LANES]
        qb = qall[:, (2 * HEADS + h) * LANES:(2 * HEADS + h + 1) * LANES]
        qm_ref[h, :, 0:LANES] = (qlat * mla_scale).astype(BF16)
        qm_ref[h, :, LANES:2 * LANES] = ((qa * cos + qb * sin) * mla_scale).astype(BF16)
    ckvn = _rms(col(2), gkv_ref[...], NORM_EPS)
    kr = col(3) * cos + col(4) * sin
    ckv_ref[...] = ckvn
    kr_ref[...] = kr[:, 0:2 * ROPE_HALF]
    kc_ref[:, 0:LANES] = ckvn.astype(BF16)
    kc_ref[:, LANES:2 * LANES] = kr.astype(BF16)

    lane = lax.broadcasted_iota(jnp.int32, (1, LANES), 1)
    for g in range(2):
        chunk = (col(5 + g) * cos + col(7 + g) * sin) * diff_scale
        rolled = pltpu.roll(chunk, LANES // 2, axis=1)
        for r in range(2):
            src = chunk if r == g else rolled
            for m in range(2):
                lo = g * 64 + m * 32
                sel = (lane >= lo) & (lane < lo + 32)
                qd_ref[g * 4 + r * 2 + m] = jnp.where(sel, src, 0.0).astype(BF16)
    dk = col(9) * cos + col(10) * sin
    dv = col(11)
    dk_ref[...] = dk
    dv_ref[...] = dv
    kd_ref[...] = dk.astype(BF16)
    vd_ref[...] = dv.astype(BF16)

    u = _gelu(col(12, 2))
    gv = _gelu(col(14, 2))
    mu = jnp.mean(gv, axis=-1, keepdims=True)
    gc = gv - mu
    var = jnp.mean(gc * gc, axis=-1, keepdims=True)
    v = gc * lax.rsqrt(var + LN_EPS) * lng_ref[...] + lnb_ref[...]
    if sample_mode:
        rest[0][...] = v
        oc_ref[...] = (u * (v * ws_ref[...] + bs_ref[...])).astype(BF16)
    else:
        lane2 = lax.broadcasted_iota(jnp.int32, (1, 2 * LANES), 1)
        row = lax.broadcasted_iota(jnp.int32, (CHUNK, CHUNK), 0)
        cl = lax.broadcasted_iota(jnp.int32, (CHUNK, CHUNK), 1)
        wts = [jnp.where(row >= cl, ws_ref[g], 0.0).astype(BF16) for g in range(4)]
        for c in range(tm // CHUNK):
            vc = v[c * CHUNK:(c + 1) * CHUNK]
            mix = bs_ref[...]
            for g in range(4):
                vg = jnp.where((lane2 >= g * 64) & (lane2 < (g + 1) * 64), vc, 0.0).astype(BF16)
                mix = mix + jnp.dot(wts[g], vg, preferred_element_type=F32)
            oc_ref[c * CHUNK:(c + 1) * CHUNK, :] = (u[c * CHUNK:(c + 1) * CHUNK] * mix).astype(BF16)


def _prep(x, cos, sin, w, *, sample_mode, tm):
    t, d = x.shape
    tm = min(tm, t)
    nin = w["win"].shape[1]
    nq = w["wq"].shape[1]
    tok = lambda n: pl.BlockSpec((tm, n), lambda i: (i, 0))
    heads = lambda n: pl.BlockSpec((HEADS, tm, n), lambda i: (0, i, 0))
    ws, bs = (w["ws_row0"], w["bs_row0"]) if sample_mode else (w["ws"], w["bs_full"])
    in_specs = [tok(d), _full((1, d)), _full((d, nin)), _full((1, 2 * LANES)), _full((2 * LANES, nq)),
                _full((1, LANES)), tok(LANES), tok(LANES), _full((1, 2 * LANES)), _full((1, 2 * LANES)),
                _full(ws.shape), _full(bs.shape)]
    out_shape = [jax.ShapeDtypeStruct((HEADS, t, 2 * LANES), BF16),
                 jax.ShapeDtypeStruct((t, 2 * LANES), BF16),
                 jax.ShapeDtypeStruct((HEADS, t, LANES), BF16),
                 jax.ShapeDtypeStruct((t, LANES), BF16),
                 jax.ShapeDtypeStruct((t, LANES), BF16),
                 jax.ShapeDtypeStruct((t, 2 * LANES), BF16),
                 jax.ShapeDtypeStruct((t, LANES), F32),
                 jax.ShapeDtypeStruct((t, 2 * ROPE_HALF), F32),
                 jax.ShapeDtypeStruct((t, LANES), F32),
                 jax.ShapeDtypeStruct((t, LANES), F32)]
    out_specs = [heads(2 * LANES), tok(2 * LANES), heads(LANES), tok(LANES), tok(LANES), tok(2 * LANES),
                 tok(LANES), tok(2 * ROPE_HALF), tok(LANES), tok(LANES)]
    if sample_mode:
        out_shape.append(jax.ShapeDtypeStruct((t, 2 * LANES), F32))
        out_specs.append(tok(2 * LANES))
    kern = functools.partial(_prep_kernel, sample_mode=sample_mode, mla_scale=w["mla_scale"],
                             diff_scale=w["diff_scale"])
    return pl.pallas_call(
        kern, grid=(t // tm,), in_specs=in_specs, out_specs=out_specs, out_shape=out_shape,
        compiler_params=_params(("parallel",)), name="prep_sample" if sample_mode else "prep_prompt",
    )(x, w["gmix"], w["win"], w["gq"], w["wq"], w["gkv"], cos, sin, w["lng"], w["lnb"], ws, bs)


def _mla_epilogue(o, wuv_ref):
    out = None
    for h in range(HEADS):
        part = jnp.dot(o[h].astype(BF16), wuv_ref[h], preferred_element_type=F32)
        out = part if out is None else out + part
    return out


def _diff_lambda(lam_ref, lam_init):
    p = lam_ref[...]
    s1 = jnp.sum(p[0:1] * p[1:2], axis=-1, keepdims=True)
    s2 = jnp.sum(p[2:3] * p[3:4], axis=-1, keepdims=True)
    return jnp.exp(s1) - jnp.exp(s2) + lam_init


def _diff_epilogue(o, lam, subln, lam_init):
    lane = lax.broadcasted_iota(jnp.int32, (1, LANES), 1)
    chunks = []
    for g in range(2):
        valid = (lane >= g * 64) & (lane < (g + 1) * 64)
        halves = []
        for r in range(2):
            d = o[g * 4 + r * 2] - lam * o[g * 4 + r * 2 + 1]
            ms = jnp.sum(jnp.where(valid, d * d, 0.0), axis=-1, keepdims=True) * (1.0 / 64.0)
            y = d * lax.rsqrt(ms + LN_EPS) * subln * (1.0 - lam_init)
            halves.append(y if r == g else pltpu.roll(y, LANES // 2, axis=1))
        chunks.append(jnp.where(lane < 64, halves[0], halves[1]))
    return chunks


def _flash_kernel(q_ref, kt_ref, v_ref, *rest, tq, tk, mode, lam_init):
    if mode == "mla":
        wuv_ref, out_ref, m_ref, l_ref, acc_ref = rest
    else:
        lam_ref, subln_ref, out_ref, m_ref, l_ref, acc_ref = rest
    i = pl.program_id(0)
    rows = HEADS * tq
    q = q_ref[...].reshape(rows, q_ref.shape[2])
    m_ref[...] = jnp.full(m_ref.shape, NEG_INF, F32)
    l_ref[...] = jnp.zeros(l_ref.shape, F32)
    acc_ref[...] = jnp.zeros(acc_ref.shape, F32)

    def step(j, masked):
        s = jnp.dot(q, kt_ref[j], preferred_element_type=F32)
        if masked:
            kpos = j * tk + lax.broadcasted_iota(jnp.int32, (1, tk), 1)
            qpos = i * tq + (lax.broadcasted_iota(jnp.int32, (rows, 1), 0) & (tq - 1))
            s = jnp.where(kpos <= qpos, s, NEG_INF)
        m_prev = m_ref[...]
        m_new = jnp.maximum(m_prev, jnp.max(s, axis=-1, keepdims=True))
        alpha = jnp.exp(m_prev - m_new)
        p = jnp.exp(s - m_new)
        l_ref[...] = alpha * l_ref[...] + jnp.sum(p, axis=-1, keepdims=True)
        v = v_ref[pl.ds(pl.multiple_of(j * tk, tk), tk), :]
        acc_ref[...] = alpha * acc_ref[...] + jnp.dot(p.astype(BF16), v, preferred_element_type=F32)
        m_ref[...] = m_new

    n_full = (i * tq) // tk

    def body(j, carry):
        step(j, False)
        return carry

    lax.fori_loop(0, n_full, body, 0)
    step(n_full, True)

    o = acc_ref[...] / l_ref[...]
    o = [o[h * tq:(h + 1) * tq] for h in range(HEADS)]
    if mode == "mla":
        out_ref[...] = _mla_epilogue(o, wuv_ref).astype(BF16)
    else:
        chunks = _diff_epilogue(o, _diff_lambda(lam_ref, lam_init), subln_ref[...], lam_init)
        out_ref[:, 0:LANES] = chunks[0].astype(BF16)
        out_ref[:, LANES:2 * LANES] = chunks[1].astype(BF16)


def _flash(q, k, v_src, extra, *, mode, lam_init, tq, tk):
    _, s, dk = q.shape
    tq, tk = min(tq, s), min(tk, s)
    assert tk % tq == 0 and s % tk == 0 and tq & (tq - 1) == 0
    kt = jnp.transpose(k.reshape(s // tk, tk, dk), (0, 2, 1))
    in_specs = [pl.BlockSpec((HEADS, tq, dk), lambda i: (0, i, 0)),
                _full(kt.shape),
                pl.BlockSpec((s, LANES), lambda i: (0, 0))]
    in_specs += [_full(e.shape) for e in extra]
    nout = 512 if mode == "mla" else 2 * LANES
    kern = functools.partial(_flash_kernel, tq=tq, tk=tk, mode=mode, lam_init=lam_init)
    return pl.pallas_call(
        kern, grid=(s // tq,), in_specs=in_specs,
        out_specs=pl.BlockSpec((tq, nout), lambda i: (i, 0)),
        out_shape=jax.ShapeDtypeStruct((s, nout), BF16),
        scratch_shapes=[pltpu.VMEM((HEADS * tq, 1), F32), pltpu.VMEM((HEADS * tq, 1), F32),
                        pltpu.VMEM((HEADS * tq, LANES), F32)],
        compiler_params=_params(("parallel",)), name="flash_" + mode,
    )(q, kt, v_src, *extra)


def _decode_kernel(pt_ref, qm_ref, qd_ref, knm_ref, knd_ref, vnd_ref, lat_ref, kr_ref, dk_ref, dv_ref,
                   om_ref, od_ref, m_ref, l_ref, acc_ref):
    p = pl.program_id(1)
    last = pl.num_programs(1) - 1

    @pl.when(p == 0)
    def _():
        m_ref[...] = jnp.full(m_ref.shape, NEG_INF, F32)
        l_ref[...] = jnp.zeros(l_ref.shape, F32)
        acc_ref[...] = jnp.zeros(acc_ref.shape, F32)

    nt = (((1,), (1,)), ((), ()))
    qm = qm_ref[...]
    qd = qd_ref[...]
    lat = lat_ref[...].astype(BF16)
    s_m = (lax.dot_general(qm[:, 0:LANES], lat, nt, preferred_element_type=F32)
           + lax.dot_general(qm[:, LANES:LANES + 2 * ROPE_HALF], kr_ref[...].astype(BF16), nt,
                             preferred_element_type=F32))
    s_d = lax.dot_general(qd, dk_ref[...].astype(BF16), nt, preferred_element_type=F32)

    def update(a, s, v):
        m_prev = m_ref[a]
        m_new = jnp.maximum(m_prev, jnp.max(s, axis=-1, keepdims=True))
        alpha = jnp.exp(m_prev - m_new)
        pr = jnp.exp(s - m_new)
        l_ref[a] = alpha * l_ref[a] + jnp.sum(pr, axis=-1, keepdims=True)
        acc_ref[a] = alpha * acc_ref[a] + jnp.dot(pr.astype(BF16), v, preferred_element_type=F32)
        m_ref[a] = m_new

    update(0, s_m, lat)
    update(1, s_d, dv_ref[...].astype(BF16))

    @pl.when(p == last)
    def _():
        def finish(a, q, kn, vn, o_ref):
            s = jnp.sum(q.astype(F32) * kn.astype(F32), axis=-1, keepdims=True)
            m_prev = m_ref[a]
            m_new = jnp.maximum(m_prev, s)
            alpha = jnp.exp(m_prev - m_new)
            pr = jnp.exp(s - m_new)
            l = alpha * l_ref[a] + pr
            acc = alpha * acc_ref[a] + pr * vn.astype(F32)
            o_ref[...] = acc / l

        finish(0, qm, knm_ref[...], knm_ref[:, 0:LANES], om_ref)
        finish(1, qd, knd_ref[...], vnd_ref[...], od_ref)


def _decode(layer, page_table, qm, qd, kn_m, kn_d, vn_d, c_lat, c_kr, c_dk, c_dv):
    b, n_pages = page_table.shape
    page = c_lat.shape[2]
    seq = lambda n, w: pl.BlockSpec((None, n, w), lambda i, p, pt: (i, 0, 0))
    cache = lambda w: pl.BlockSpec((None, None, page, w), lambda i, p, pt: (layer, pt[i, p], 0, 0))
    grid_spec = pltpu.PrefetchScalarGridSpec(
        num_scalar_prefetch=1, grid=(b, n_pages),
        in_specs=[seq(HEADS, 2 * LANES), seq(HEADS, LANES), seq(1, 2 * LANES), seq(1, LANES), seq(1, LANES),
                  cache(LANES), cache(c_kr.shape[3]), cache(LANES), cache(LANES)],
        out_specs=[seq(HEADS, LANES), seq(HEADS, LANES)],
        scratch_shapes=[pltpu.VMEM((2, HEADS, 1), F32), pltpu.VMEM((2, HEADS, 1), F32),
                        pltpu.VMEM((2, HEADS, LANES), F32)])
    return pl.pallas_call(
        _decode_kernel, grid_spec=grid_spec,
        out_shape=[jax.ShapeDtypeStruct((b, HEADS, LANES), F32), jax.ShapeDtypeStruct((b, HEADS, LANES), F32)],
        compiler_params=_params(("parallel", "arbitrary")), name="decode",
    )(page_table, qm, qd, kn_m, kn_d, vn_d, c_lat, c_kr, c_dk, c_dv)


def _sample_epilogue_kernel(om_ref, od_ref, wuv_ref, lam_ref, subln_ref, omla_ref, odiff_ref, *, lam_init):
    om = [om_ref[h] for h in range(HEADS)]
    od = [od_ref[h] for h in range(HEADS)]
    omla_ref[...] = _mla_epilogue(om, wuv_ref).astype(BF16)
    chunks = _diff_epilogue(od, _diff_lambda(lam_ref, lam_init), subln_ref[...], lam_init)
    odiff_ref[:, 0:LANES] = chunks[0].astype(BF16)
    odiff_ref[:, LANES:2 * LANES] = chunks[1].astype(BF16)


def _sample_epilogue(om, od, w, lam_init):
    t = om.shape[1]
    return pl.pallas_call(
        functools.partial(_sample_epilogue_kernel, lam_init=lam_init),
        out_shape=[jax.ShapeDtypeStruct((t, 512), BF16), jax.ShapeDtypeStruct((t, 2 * LANES), BF16)],
        name="sample_epilogue",
    )(om, od, w["wuv_pad"], w["lam"], w["subln"])


def _outproj_kernel(x_ref, a_ref, b_ref, c_ref, w_ref, g_ref, xo_ref, h_ref):
    na, nb = a_ref.shape[1], b_ref.shape[1]
    y = (jnp.dot(a_ref[...], w_ref[0:na, :], preferred_element_type=F32)
         + jnp.dot(b_ref[...], w_ref[na:na + nb, :], preferred_element_type=F32)
         + jnp.dot(c_ref[...], w_ref[na + nb:, :], preferred_element_type=F32))
    xo = x_ref[...] + y
    xo_ref[...] = xo
    h_ref[...] = _rms(xo, g_ref[...], NORM_EPS).astype(BF16)


def _outproj(x, a, b, c, w_out, g, *, tm):
    t, d = x.shape
    tm = min(tm, t)
    tok = lambda n: pl.BlockSpec((tm, n), lambda i: (i, 0))
    return pl.pallas_call(
        _outproj_kernel, grid=(t // tm,),
        in_specs=[tok(d), tok(a.shape[1]), tok(b.shape[1]), tok(c.shape[1]), _full(w_out.shape), _full((1, d))],
        out_specs=[tok(d), tok(d)],
        out_shape=[jax.ShapeDtypeStruct((t, d), F32), jax.ShapeDtypeStruct((t, d), BF16)],
        compiler_params=_params(("parallel",)), name="outproj",
    )(x, a, b, c, w_out, g)


def _swiglu_tile(h, wg, wu, wd):
    g = jnp.dot(h, wg, preferred_element_type=F32)
    u = jnp.dot(h, wu, preferred_element_type=F32)
    act = (g * jax.nn.sigmoid(g) * u).astype(BF16)
    return jnp.dot(act, wd, preferred_element_type=F32)


def _ffn_kernel(x_ref, h_ref, wg_ref, wu_ref, wd_ref, gf_ref, o_ref, acc_ref, *, final_norm):
    f = pl.program_id(1)

    @pl.when(f == 0)
    def _():
        acc_ref[...] = jnp.zeros(acc_ref.shape, F32)

    acc_ref[...] += _swiglu_tile(h_ref[...], wg_ref[...], wu_ref[...], wd_ref[...])

    @pl.when(f == pl.num_programs(1) - 1)
    def _():
        y = x_ref[...] + acc_ref[...]
        o_ref[...] = _rms(y, gf_ref[...], NORM_EPS) if final_norm else y


def _ffn(x, h, wg, wu, wd, g_final, *, final_norm, tm, tf):
    t, d = x.shape
    dff = wg.shape[1]
    tm = min(tm, t)
    assert dff % tf == 0
    return pl.pallas_call(
        functools.partial(_ffn_kernel, final_norm=final_norm), grid=(t // tm, dff // tf),
        in_specs=[pl.BlockSpec((tm, d), lambda i, f: (i, 0)), pl.BlockSpec((tm, d), lambda i, f: (i, 0)),
                  pl.BlockSpec((d, tf), lambda i, f: (0, f)), pl.BlockSpec((d, tf), lambda i, f: (0, f)),
                  pl.BlockSpec((tf, d), lambda i, f: (f, 0)), pl.BlockSpec((1, d), lambda i, f: (0, 0))],
        out_specs=pl.BlockSpec((tm, d), lambda i, f: (i, 0)),
        out_shape=jax.ShapeDtypeStruct((t, d), F32),
        scratch_shapes=[pltpu.VMEM((tm, d), F32)],
        compiler_params=_params(("parallel", "arbitrary")), name="ffn",
    )(x, h, wg, wu, wd, g_final)


def _router_kernel(x_ref, g_ref, wr_ref, gate_ref, *, n_experts):
    h = _rms(x_ref[...], g_ref[...], NORM_EPS)
    logits = jnp.dot(h, wr_ref[...], precision=lax.Precision.HIGHEST, preferred_element_type=F32)
    lane = lax.broadcasted_iota(jnp.int32, logits.shape, 1).astype(F32)
    logits = jnp.where(lane < n_experts, logits, NEG_INF)
    gate = jnp.zeros(logits.shape, F32)
    tops, picks = [], []
    for _ in range(TOP_K):
        top = jnp.max(logits, axis=-1, keepdims=True)
        pick = jnp.min(jnp.where(logits == top, lane, float(LANES)), axis=-1, keepdims=True)
        tops.append(top)
        picks.append(pick)
        logits = jnp.where(lane == pick, NEG_INF, logits)
    e = [jnp.exp(tv - tops[0]) for tv in tops]
    den = e[0]
    for ev in e[1:]:
        den = den + ev
    for ev, pick in zip(e, picks):
        gate = gate + jnp.where(lane == pick, ev / den, 0.0)
    gate_ref[...] = gate


def _router(x, g, wr_pad, n_experts, *, tm):
    t, d = x.shape
    tm = min(tm, t)
    return pl.pallas_call(
        functools.partial(_router_kernel, n_experts=n_experts), grid=(t // tm,),
        in_specs=[pl.BlockSpec((tm, d), lambda i: (i, 0)), _full((1, d)), _full(wr_pad.shape)],
        out_specs=pl.BlockSpec((tm, LANES), lambda i: (i, 0)),
        out_shape=jax.ShapeDtypeStruct((t, LANES), F32),
        compiler_params=_params(("parallel",)), name="router",
    )(x, g, wr_pad)


def _moe_kernel(x_ref, h_ref, gate_ref, wg_ref, wu_ref, wd_ref, gf_ref, o_ref, acc_ref, *, final_norm):
    e = pl.program_id(1)
    f = pl.program_id(2)

    @pl.when((e == 0) & (f == 0))
    def _():
        acc_ref[...] = jnp.zeros(acc_ref.shape, F32)

    lane = lax.broadcasted_iota(jnp.int32, (1, LANES), 1)
    ge = jnp.sum(jnp.where(lane == e, gate_ref[...], 0.0), axis=-1, keepdims=True)
    acc_ref[...] += ge * _swiglu_tile(h_ref[...], wg_ref[...], wu_ref[...], wd_ref[...])

    @pl.when((e == pl.num_programs(1) - 1) & (f == pl.num_programs(2) - 1))
    def _():
        y = x_ref[...] + acc_ref[...]
        o_ref[...] = _rms(y, gf_ref[...], NORM_EPS) if final_norm else y


def _moe(x, h, gate, wg, wu, wd, g_final, *, final_norm, tm, tf):
    t, d = x.shape
    n_e, _, dff = wg.shape
    tm = min(tm, t)
    assert dff % tf == 0
    return pl.pallas_call(
        functools.partial(_moe_kernel, final_norm=final_norm), grid=(t // tm, n_e, dff // tf),
        in_specs=[pl.BlockSpec((tm, d), lambda i, e, f: (i, 0)), pl.BlockSpec((tm, d), lambda i, e, f: (i, 0)),
                  pl.BlockSpec((tm, LANES), lambda i, e, f: (i, 0)),
                  pl.BlockSpec((None, d, tf), lambda i, e, f: (e, 0, f)),
                  pl.BlockSpec((None, d, tf), lambda i, e, f: (e, 0, f)),
                  pl.BlockSpec((None, tf, d), lambda i, e, f: (e, f, 0)),
                  pl.BlockSpec((1, d), lambda i, e, f: (0, 0))],
        out_specs=pl.BlockSpec((tm, d), lambda i, e, f: (i, 0)),
        out_shape=jax.ShapeDtypeStruct((t, d), F32),
        scratch_shapes=[pltpu.VMEM((tm, d), F32)],
        compiler_params=_params(("parallel", "arbitrary", "arbitrary")), name="moe",
    )(x, h, gate, wg, wu, wd, g_final)


def _rot_cols(w):
    return jnp.concatenate([-w[..., ROPE_HALF:], w[..., :ROPE_HALF]], axis=-1)


def _pad_cols(w, n):
    return jnp.pad(w, ((0, 0), (0, n - w.shape[1])))


def _layer_weights(l, P):
    d = P["w_in"].shape[1]
    cq_n = P["mla_q_norm"].shape[1]
    ckv_n = P["mla_kv_norm"].shape[1]
    heads, nope = P["mla_w_uk"].shape[2], P["mla_w_uk"].shape[3]
    rope_n = P["mla_w_uq"].shape[3] - nope
    v_n = P["mla_w_uv"].shape[3]
    dh = P["diff_lambda_q1"].shape[1]
    n_groups, chunk = P["chunk_w_s"].shape[1], P["chunk_w_s"].shape[2]
    cw = P["chunk_ln_g"].shape[1]
    assert (cq_n, ckv_n, heads, rope_n, dh, n_groups, chunk, cw) == (256, 128, HEADS, 32, 32, 4, CHUNK, 256)
    dq_n, dk_n, dv_n = 256, 128, 128
    assert P["w_in"].shape[2] == cq_n + ckv_n + rope_n + dq_n + dk_n + dv_n + 2 * cw
    w_in = P["w_in"][l]
    o = 0
    parts = {}
    for name, n in (("cq", cq_n), ("ckv", ckv_n), ("kr", rope_n), ("dq", dq_n), ("dk", dk_n), ("dv", dv_n),
                    ("cu", cw), ("cv", cw)):
        parts[name] = w_in[:, o:o + n]
        o += n
    rot32 = lambda w: _rot_cols(w.reshape(d, -1, 2 * ROPE_HALF)).reshape(d, -1)
    win = jnp.concatenate([
        parts["cq"], parts["ckv"], _pad_cols(parts["kr"], LANES), _pad_cols(rot32(parts["kr"]), LANES),
        parts["dq"], rot32(parts["dq"]), parts["dk"], rot32(parts["dk"]), parts["dv"], parts["cu"], parts["cv"],
    ], axis=1).astype(BF16)

    uq = jnp.transpose(P["mla_w_uq"][l], (1, 0, 2))
    uk = jnp.transpose(P["mla_w_uk"][l], (1, 0, 2))
    wq_lat = _fold_q(uq[:, :, :nope], uk)
    uq_r = uq[:, :, nope:]
    pad_r = lambda w: jnp.transpose(jnp.pad(w, ((0, 0), (0, 0), (0, LANES - rope_n))), (1, 0, 2)).reshape(cq_n, -1)
    wq = jnp.concatenate([wq_lat, pad_r(uq_r), pad_r(_rot_cols(uq_r))], axis=1).astype(BF16)

    uv = jnp.transpose(P["mla_w_uv"][l], (1, 0, 2))
    wuv_pad = jnp.stack([jnp.pad(uv[h], ((0, 0), (h * v_n, (heads - 1 - h) * v_n))) for h in range(heads)])

    ws = P["chunk_w_s"][l]
    bs = P["chunk_b_s"][l]
    rep = lambda a: jnp.repeat(a, cw // n_groups, axis=-1)
    return dict(
        win=win, wq=wq, wuv_pad=wuv_pad.astype(BF16),
        gmix=P["norm_mix"][l][None], gq=P["mla_q_norm"][l][None], gkv=P["mla_kv_norm"][l][None],
        lng=P["chunk_ln_g"][l][None], lnb=P["chunk_ln_b"][l][None],
        ws=ws, bs_full=rep(jnp.transpose(bs)),
        ws_row0=rep(ws[:, 0, 0][None]), bs_row0=rep(bs[:, 0][None]),
        lam=jnp.stack([P["diff_lambda_q1"][l], P["diff_lambda_k1"][l], P["diff_lambda_q2"][l],
                       P["diff_lambda_k2"][l]]),
        subln=jnp.tile(P["diff_subln"][l], 2)[None],
        w_out=P["w_out"][l].astype(BF16), gffn=P["norm_ffn"][l][None],
        mla_scale=float((nope + rope_n) ** -0.5), diff_scale=float(dh ** -0.5),
    )


def _rope_tables(pos):
    inv = 1.0 / (ROPE_THETA ** (jnp.arange(ROPE_HALF, dtype=F32) / ROPE_HALF))
    ang = pos.astype(F32)[:, None] * inv[None, :]
    reps = LANES // ROPE_HALF
    return jnp.tile(jnp.cos(ang), (1, reps)), jnp.tile(jnp.sin(ang), (1, reps))


def kernel(x_prompt, x_sample, cache_mla_latent, cache_mla_krope, cache_diff_k, cache_diff_v, page_table, norm_mix, w_in, mla_q_norm, mla_w_uq, mla_kv_norm, mla_w_uk, mla_w_uv, diff_lambda_q1, diff_lambda_k1, diff_lambda_q2, diff_lambda_k2, diff_subln, chunk_ln_g, chunk_ln_b, chunk_w_s, chunk_b_s, w_out, norm_ffn, ffn_w_gate, ffn_w_up, ffn_w_down, moe_w_router, moe_w_gate, moe_w_up, moe_w_down, norm_final):
    P = dict(norm_mix=norm_mix, w_in=w_in, mla_q_norm=mla_q_norm, mla_w_uq=mla_w_uq, mla_kv_norm=mla_kv_norm,
             mla_w_uk=mla_w_uk, mla_w_uv=mla_w_uv, diff_lambda_q1=diff_lambda_q1, diff_lambda_k1=diff_lambda_k1,
             diff_lambda_q2=diff_lambda_q2, diff_lambda_k2=diff_lambda_k2, diff_subln=diff_subln,
             chunk_ln_g=chunk_ln_g, chunk_ln_b=chunk_ln_b, chunk_w_s=chunk_w_s, chunk_b_s=chunk_b_s,
             w_out=w_out, norm_ffn=norm_ffn)
    depth = w_in.shape[0]
    bp, s, d = x_prompt.shape
    bs, ts, _ = x_sample.shape
    assert bp == 1 and ts == 1
    n_pool, page = cache_mla_latent.shape[1], cache_mla_latent.shape[2]
    past_len = page_table.shape[1] * page
    assert past_len % CHUNK == 0
    n_experts = moe_w_router.shape[2]

    cos_p, sin_p = _rope_tables(jnp.arange(s))
    cos_s, sin_s = _rope_tables(jnp.full((bs,), past_len))
    c_dk = cache_diff_k.reshape(depth, n_pool, page, LANES)
    c_dv = cache_diff_v.reshape(depth, n_pool, page, LANES)
    gfin = norm_final[None]

    xp = x_prompt.reshape(s, d)
    xs = x_sample.reshape(bs, d)
    st_p, st_s = [], []
    for l in range(depth):
        w = _layer_weights(l, P)
        lam_init = 0.8 - 0.6 * math.exp(-0.3 * l)
        last = l == depth - 1
        j = l // 2
        if l % 2 == 0:
            mix_w = (ffn_w_gate[j].astype(BF16), ffn_w_up[j].astype(BF16), ffn_w_down[j].astype(BF16))
        else:
            mix_w = (moe_w_gate[j].astype(BF16), moe_w_up[j].astype(BF16), moe_w_down[j].astype(BF16))
            wr_pad = _pad_cols(moe_w_router[j], LANES)

        def channel_mix(x, h, tm):
            if l % 2 == 0:
                return _ffn(x, h, *mix_w, gfin, final_norm=last, tm=tm, tf=mix_w[0].shape[1] // 2)
            gate = _router(x, w["gffn"], wr_pad, n_experts, tm=tm)
            return _moe(x, h, gate, *mix_w, gfin, final_norm=last, tm=tm, tf=mix_w[0].shape[2] // 2)

        qm, kc, qd, kd, vd, oc, ckv, kr, dk, dv = _prep(xp, cos_p, sin_p, w, sample_mode=False, tm=256)
        o_mla = _flash(qm, kc, kc, (w["wuv_pad"],), mode="mla", lam_init=lam_init, tq=128, tk=512)
        o_diff = _flash(qd, kd, vd, (w["lam"], w["subln"]), mode="diff", lam_init=lam_init, tq=128, tk=512)
        xp, hp = _outproj(xp, o_mla, o_diff, oc, w["w_out"], w["gffn"], tm=512)
        xp = channel_mix(xp, hp, 512)
        st_p.append((ckv, kr, dk, dv))

        qm, kc, qd, kd, vd, oc, ckv, kr, dk, dv, vst = _prep(xs, cos_s, sin_s, w, sample_mode=True, tm=128)
        om, od = _decode(l, page_table, jnp.transpose(qm, (1, 0, 2)), jnp.transpose(qd, (1, 0, 2)),
                         kc[:, None], kd[:, None], vd[:, None],
                         cache_mla_latent, cache_mla_krope, c_dk, c_dv)
        o_mla, o_diff = _sample_epilogue(jnp.transpose(om, (1, 0, 2)), jnp.transpose(od, (1, 0, 2)), w, lam_init)
        xs, hs = _outproj(xs, o_mla, o_diff, oc, w["w_out"], w["gffn"], tm=128)
        xs = channel_mix(xs, hs, 128)
        st_s.append((ckv, kr, dk, dv, vst))

    stack = lambda sts, i, shape: jnp.stack([st[i] for st in sts], axis=0).reshape(shape)
    return (xp.reshape(bp, s, d), xs.reshape(bs, ts, d),
            stack(st_p, 0, (depth, bp, s, -1)), stack(st_p, 1, (depth, bp, s, -1)),
            stack(st_p, 2, (depth, bp, s) + cache_diff_k.shape[3:]),
            stack(st_p, 3, (depth, bp, s) + cache_diff_v.shape[3:]),
            stack(st_s, 0, (depth, bs, ts, -1)), stack(st_s, 1, (depth, bs, ts, -1)),
            stack(st_s, 2, (depth, bs, ts) + cache_diff_k.shape[3:]),
            stack(st_s, 3, (depth, bs, ts) + cache_diff_v.shape[3:]),
            stack(st_s, 4, (depth, bs, ts, -1)))
```

```python
import functools
import math

import jax
import jax.numpy as jnp
from jax import lax
from jax.experimental import pallas as pl
from jax.experimental.pallas import tpu as pltpu

F32 = jnp.float32
BF16 = jnp.bfloat16

ROPE_THETA = 10000.0
NORM_EPS = 1e-6
LN_EPS = 1e-5
NEG_INF = -1e30
TOP_K = 2
LOG2E = math.log2(math.e)

LANES = 128
HEADS = 8
ROPE_HALF = 16
CHUNK = 128
VMEM_LIMIT = 56 * 1024 * 1024


def _params(sem):
    return pltpu.CompilerParams(dimension_semantics=sem, vmem_limit_bytes=VMEM_LIMIT)


def _rms(x, g, eps):
    return x * lax.rsqrt(jnp.mean(x * x, axis=-1, keepdims=True) + eps) * g


def _gelu(x):
    return 0.5 * x * (1.0 + jnp.tanh(math.sqrt(2.0 / math.pi) * (x + 0.044715 * (x * x * x))))


def _full(shape):
    return pl.BlockSpec(shape, lambda *_: (0,) * len(shape))


def _fold_q_kernel(uq_ref, uk_ref, o_ref):
    for h in range(HEADS):
        o_ref[:, h * LANES:(h + 1) * LANES] = lax.dot_general(
            uq_ref[h], uk_ref[h], (((1,), (1,)), ((), ())),
            precision=lax.Precision.HIGHEST, preferred_element_type=F32)


def _fold_q(uq_nope, uk):
    h, cq, _ = uq_nope.shape
    return pl.pallas_call(
        _fold_q_kernel,
        out_shape=jax.ShapeDtypeStruct((cq, h * uk.shape[1]), F32),
        name="fold_q",
    )(uq_nope, uk)


def _prep_kernel(x_ref, gmix_ref, win_ref, gq_ref, wq_ref, gkv_ref, cos_ref, sin_ref, lng_ref, lnb_ref,
                 ws_ref, bs_ref,
                 qm_ref, kc_ref, qd_ref, kd_ref, vd_ref, oc_ref, ckv_ref, kr_ref, dk_ref, dv_ref, *rest,
                 sample_mode, mla_scale, diff_scale):
    tm = x_ref.shape[0]
    x = x_ref[...]
    xn = _rms(x, gmix_ref[...], NORM_EPS).astype(BF16)
    z = jnp.dot(xn, win_ref[...], preferred_element_type=F32)
    cos = cos_ref[...]
    sin = sin_ref[...]
    col = lambda a, n=1: z[:, a * LANES:(a + n) * LANES]

    cqn = _rms(col(0, 2), gq_ref[...], NORM_EPS).astype(BF16)
    qall = jnp.dot(cqn, wq_ref[...], preferred_element_type=F32)
    for h in range(HEADS):
        qlat = qall[:, h * LANES:(h + 1) * LANES]
        qa = qall[:, (HEADS + h) * LANES:(HEADS + h + 1) * LANES]
        qb = qall[:, (2 * HEADS + h) * LANES:(2 * HEADS + h + 1) * LANES]
        qm_ref[h, :, 0:LANES] = (qlat * mla_scale).astype(BF16)
        qm_ref[h, :, LANES:2 * LANES] = ((qa * cos + qb * sin) * mla_scale).astype(BF16)
    ckvn = _rms(col(2), gkv_ref[...], NORM_EPS)
    kr = col(3) * cos + col(4) * sin
    ckv_ref[...] = ckvn
    kr_ref[...] = kr[:, 0:2 * ROPE_HALF]
    kc_ref[:, 0:LANES] = ckvn.astype(BF16)
    kc_ref[:, LANES:2 * LANES] = kr.astype(BF16)

    lane = lax.broadcasted_iota(jnp.int32, (1, LANES), 1)
    for g in range(2):
        chunk = (col(5 + g) * cos + col(7 + g) * sin) * diff_scale
        rolled = pltpu.roll(chunk, LANES // 2, axis=1)
        for r in range(2):
            src = chunk if r == g else rolled
            for m in range(2):
                lo = g * 64 + m * 32
                sel = (lane >= lo) & (lane < lo + 32)
                qd_ref[g * 4 + r * 2 + m] = jnp.where(sel, src, 0.0).astype(BF16)
    dk = col(9) * cos + col(10) * sin
    dv = col(11)
    dk_ref[...] = dk
    dv_ref[...] = dv
    kd_ref[...] = dk.astype(BF16)
    vd_ref[...] = dv.astype(BF16)

    u = _gelu(col(12, 2))
    gv = _gelu(col(14, 2))
    mu = jnp.mean(gv, axis=-1, keepdims=True)
    gc = gv - mu
    var = jnp.mean(gc * gc, axis=-1, keepdims=True)
    v = gc * lax.rsqrt(var + LN_EPS) * lng_ref[...] + lnb_ref[...]
    if sample_mode:
        rest[0][...] = v
        oc_ref[...] = (u * (v * ws_ref[...] + bs_ref[...])).astype(BF16)
    else:
        vme_ref, vde_ref = rest
        ones_col = jnp.broadcast_to(jnp.where(lane == 0, 1.0, 0.0), (tm, LANES)).astype(BF16)
        vme_ref[:, 0:LANES] = ckvn.astype(BF16)
        vme_ref[:, LANES:2 * LANES] = ones_col
        vde_ref[:, 0:LANES] = dv.astype(BF16)
        vde_ref[:, LANES:2 * LANES] = ones_col
        lane2 = lax.broadcasted_iota(jnp.int32, (1, 2 * LANES), 1)
        row = lax.broadcasted_iota(jnp.int32, (CHUNK, CHUNK), 0)
        cl = lax.broadcasted_iota(jnp.int32, (CHUNK, CHUNK), 1)
        wts = [jnp.where(row >= cl, ws_ref[g], 0.0).astype(BF16) for g in range(4)]
        for c in range(tm // CHUNK):
            vc = v[c * CHUNK:(c + 1) * CHUNK]
            mix = bs_ref[...]
            for g in range(4):
                vg = jnp.where((lane2 >= g * 64) & (lane2 < (g + 1) * 64), vc, 0.0).astype(BF16)
                mix = mix + jnp.dot(wts[g], vg, preferred_element_type=F32)
            oc_ref[c * CHUNK:(c + 1) * CHUNK, :] = (u[c * CHUNK:(c + 1) * CHUNK] * mix).astype(BF16)


def _prep(x, cos, sin, w, *, sample_mode, tm):
    t, d = x.shape
    tm = min(tm, t)
    nin = w["win"].shape[1]
    nq = w["wq"].shape[1]
    tok = lambda n: pl.BlockSpec((tm, n), lambda i: (i, 0))
    heads = lambda n: pl.BlockSpec((HEADS, tm, n), lambda i: (0, i, 0))
    ws, bs = (w["ws_row0"], w["bs_row0"]) if sample_mode else (w["ws"], w["bs_full"])
    in_specs = [tok(d), _full((1, d)), _full((d, nin)), _full((1, 2 * LANES)), _full((2 * LANES, nq)),
                _full((1, LANES)), tok(LANES), tok(LANES), _full((1, 2 * LANES)), _full((1, 2 * LANES)),
                _full(ws.shape), _full(bs.shape)]
    out_shape = [jax.ShapeDtypeStruct((HEADS, t, 2 * LANES), BF16),
                 jax.ShapeDtypeStruct((t, 2 * LANES), BF16),
                 jax.ShapeDtypeStruct((HEADS, t, LANES), BF16),
                 jax.ShapeDtypeStruct((t, LANES), BF16),
                 jax.ShapeDtypeStruct((t, LANES), BF16),
                 jax.ShapeDtypeStruct((t, 2 * LANES), BF16),
                 jax.ShapeDtypeStruct((t, LANES), F32),
                 jax.ShapeDtypeStruct((t, 2 * ROPE_HALF), F32),
                 jax.ShapeDtypeStruct((t, LANES), F32),
                 jax.ShapeDtypeStruct((t, LANES), F32)]
    out_specs = [heads(2 * LANES), tok(2 * LANES), heads(LANES), tok(LANES), tok(LANES), tok(2 * LANES),
                 tok(LANES), tok(2 * ROPE_HALF), tok(LANES), tok(LANES)]
    if sample_mode:
        out_shape.append(jax.ShapeDtypeStruct((t, 2 * LANES), F32))
        out_specs.append(tok(2 * LANES))
    else:
        out_shape += [jax.ShapeDtypeStruct((t, 2 * LANES), BF16)] * 2
        out_specs += [tok(2 * LANES)] * 2
    kern = functools.partial(_prep_kernel, sample_mode=sample_mode, mla_scale=w["mla_scale"] * LOG2E,
                             diff_scale=w["diff_scale"] * LOG2E)
    return pl.pallas_call(
        kern, grid=(t // tm,), in_specs=in_specs, out_specs=out_specs, out_shape=out_shape,
        compiler_params=_params(("parallel",)), name="prep_sample" if sample_mode else "prep_prompt",
    )(x, w["gmix"], w["win"], w["gq"], w["wq"], w["gkv"], cos, sin, w["lng"], w["lnb"], ws, bs)


def _mla_epilogue(o, wuv_ref):
    out = None
    for h in range(HEADS):
        part = jnp.dot(o[h].astype(BF16), wuv_ref[h], preferred_element_type=F32)
        out = part if out is None else out + part
    return out


def _diff_lambda(lam_ref, lam_init):
    p = lam_ref[...]
    s1 = jnp.sum(p[0:1] * p[1:2], axis=-1, keepdims=True)
    s2 = jnp.sum(p[2:3] * p[3:4], axis=-1, keepdims=True)
    return jnp.exp(s1) - jnp.exp(s2) + lam_init


def _diff_epilogue(o, lam, subln, lam_init):
    lane = lax.broadcasted_iota(jnp.int32, (1, LANES), 1)
    chunks = []
    for g in range(2):
        valid = (lane >= g * 64) & (lane < (g + 1) * 64)
        halves = []
        for r in range(2):
            d = o[g * 4 + r * 2] - lam * o[g * 4 + r * 2 + 1]
            ms = jnp.sum(jnp.where(valid, d * d, 0.0), axis=-1, keepdims=True) * (1.0 / 64.0)
            y = d * lax.rsqrt(ms + LN_EPS) * subln * (1.0 - lam_init)
            halves.append(y if r == g else pltpu.roll(y, LANES // 2, axis=1))
        chunks.append(jnp.where(lane < 64, halves[0], halves[1]))
    return chunks


def _flash_kernel(q_ref, kt_ref, v_ref, *rest, tq, tk, groups, mode, lam_init):
    if mode == "mla":
        wuv_ref, out_ref, m_ref, acc_ref = rest
    else:
        lam_ref, subln_ref, out_ref, m_ref, acc_ref = rest
    i = pl.program_id(0)
    hg = HEADS // groups
    rows = hg * tq
    m_ref[...] = jnp.full(m_ref.shape, NEG_INF, F32)
    acc_ref[...] = jnp.zeros(acc_ref.shape, F32)

    def step(j, masked):
        kt = kt_ref[j]
        v = v_ref[pl.ds(pl.multiple_of(j * tk, tk), tk), :]
        if masked:
            kpos = j * tk + lax.broadcasted_iota(jnp.int32, (1, tk), 1)
            qpos = i * tq + (lax.broadcasted_iota(jnp.int32, (rows, 1), 0) & (tq - 1))
            keep = kpos <= qpos
        for g in range(groups):
            q = q_ref[g * hg:(g + 1) * hg].reshape(rows, q_ref.shape[2])
            s = jnp.dot(q, kt, preferred_element_type=F32)
            if masked:
                s = jnp.where(keep, s, NEG_INF)
            m_prev = m_ref[g]
            m_new = jnp.maximum(m_prev, jnp.max(s, axis=-1, keepdims=True))
            alpha = jnp.exp2(m_prev - m_new)
            p = jnp.exp2(s - m_new).astype(BF16)
            acc_ref[g] = alpha * acc_ref[g] + jnp.dot(p, v, preferred_element_type=F32)
            m_ref[g] = m_new

    n_full = (i * tq) // tk

    def body(j, carry):
        step(j, False)
        return carry

    lax.fori_loop(0, n_full, body, 0)
    step(n_full, True)

    o = []
    for g in range(groups):
        acc = acc_ref[g]
        og = acc[:, 0:LANES] / acc[:, LANES:LANES + 1]
        o += [og[h * tq:(h + 1) * tq] for h in range(hg)]
    if mode == "mla":
        out_ref[...] = _mla_epilogue(o, wuv_ref).astype(BF16)
    else:
        chunks = _diff_epilogue(o, _diff_lambda(lam_ref, lam_init), subln_ref[...], lam_init)
        out_ref[:, 0:LANES] = chunks[0].astype(BF16)
        out_ref[:, LANES:2 * LANES] = chunks[1].astype(BF16)


def _flash(q, k, v_ext, extra, *, mode, lam_init, tq, tk, groups):
    _, s, dk = q.shape
    tq, tk = min(tq, s), min(tk, s)
    assert tk % tq == 0 and s % tk == 0 and tq & (tq - 1) == 0 and HEADS % groups == 0
    kt = jnp.transpose(k.reshape(s // tk, tk, dk), (0, 2, 1))
    in_specs = [pl.BlockSpec((HEADS, tq, dk), lambda i: (0, i, 0)), _full(kt.shape), _full(v_ext.shape)]
    in_specs += [_full(e.shape) for e in extra]
    nout = 512 if mode == "mla" else 2 * LANES
    rows = HEADS // groups * tq
    kern = functools.partial(_flash_kernel, tq=tq, tk=tk, groups=groups, mode=mode, lam_init=lam_init)
    return pl.pallas_call(
        kern, grid=(s // tq,), in_specs=in_specs,
        out_specs=pl.BlockSpec((tq, nout), lambda i: (i, 0)),
        out_shape=jax.ShapeDtypeStruct((s, nout), BF16),
        scratch_shapes=[pltpu.VMEM((groups, rows, 1), F32), pltpu.VMEM((groups, rows, 2 * LANES), F32)],
        compiler_params=_params(("parallel",)), name="flash_" + mode,
    )(q, kt, v_ext, *extra)


def _decode_kernel(pt_ref, qm_ref, qd_ref, knm_ref, knd_ref, vnd_ref, lat_hbm, kr_hbm, dk_hbm, dv_hbm,
                   om_ref, od_ref, lat_buf, kr_buf, dk_buf, dv_buf, sem, m_ref, l_ref, acc_ref,
                   *, layer, n_chunks, cp, page):
    t = pl.program_id(0)
    c = t % n_chunks
    slot = t % 2

    def chunk_copies(step, slot):
        b = step // n_chunks
        first = (step % n_chunks) * cp
        out = []
        for i in range(cp):
            pg = pt_ref[b, first + i]
            tok = pl.ds(i * page, page)
            out.append(pltpu.make_async_copy(lat_hbm.at[layer, pg], lat_buf.at[slot, tok, :], sem.at[slot]))
            out.append(pltpu.make_async_copy(kr_hbm.at[layer, pg], kr_buf.at[slot, :, tok], sem.at[slot]))
            out.append(pltpu.make_async_copy(dk_hbm.at[layer, pg], dk_buf.at[slot, :, tok], sem.at[slot]))
            out.append(pltpu.make_async_copy(dv_hbm.at[layer, pg], dv_buf.at[slot, :, tok], sem.at[slot]))
        return out

    @pl.when(t == 0)
    def _():
        for cpy in chunk_copies(0, 0):
            cpy.start()

    @pl.when(t + 1 < pl.num_programs(0))
    def _():
        for cpy in chunk_copies(t + 1, 1 - slot):
            cpy.start()

    for cpy in chunk_copies(t, slot):
        cpy.wait()

    @pl.when(c == 0)
    def _():
        m_ref[...] = jnp.full(m_ref.shape, NEG_INF, F32)
        l_ref[...] = jnp.zeros(l_ref.shape, F32)
        acc_ref[...] = jnp.zeros(acc_ref.shape, F32)

    nt = (((1,), (1,)), ((), ()))
    qm = qm_ref[...]
    qd = qd_ref[...]
    lat = lat_buf[slot].astype(BF16)
    s_m = (lax.dot_general(qm[:, 0:LANES], lat, nt, preferred_element_type=F32)
           + jnp.dot(qm[:, LANES:LANES + 2 * ROPE_HALF], kr_buf[slot].astype(BF16), preferred_element_type=F32))
    s_d = jnp.dot(qd, dk_buf[slot].astype(BF16), preferred_element_type=F32)

    def update(a, s, pv):
        m_prev = m_ref[a]
        m_new = jnp.maximum(m_prev, jnp.max(s, axis=-1, keepdims=True))
        alpha = jnp.exp2(m_prev - m_new)
        pr = jnp.exp2(s - m_new)
        l_ref[a] = alpha * l_ref[a] + jnp.sum(pr, axis=-1, keepdims=True)
        acc_ref[a] = alpha * acc_ref[a] + pv(pr.astype(BF16))
        m_ref[a] = m_new

    update(0, s_m, lambda pr: jnp.dot(pr, lat, preferred_element_type=F32))
    update(1, s_d, lambda pr: lax.dot_general(pr, dv_buf[slot].astype(BF16), nt, preferred_element_type=F32))

    @pl.when(c == n_chunks - 1)
    def _():
        def finish(a, q, kn, vn, o_ref):
            s = jnp.sum(q.astype(F32) * kn.astype(F32), axis=-1, keepdims=True)
            m_prev = m_ref[a]
            m_new = jnp.maximum(m_prev, s)
            alpha = jnp.exp2(m_prev - m_new)
            pr = jnp.exp2(s - m_new)
            l = alpha * l_ref[a] + pr
            acc = alpha * acc_ref[a] + pr * vn.astype(F32)
            o_ref[...] = acc / l

        finish(0, qm, knm_ref[...], knm_ref[:, 0:LANES], om_ref)
        finish(1, qd, knd_ref[...], vnd_ref[...], od_ref)


def _decode(layer, page_table, qm, qd, kn_m, kn_d, vn_d, c_lat, c_krt, c_dkt, c_dvt, *, cp):
    b, n_pages = page_table.shape
    page = c_lat.shape[2]
    cp = min(cp, n_pages)
    assert n_pages % cp == 0
    n_chunks = n_pages // cp
    keys = cp * page
    seq = lambda n, w: pl.BlockSpec((None, n, w), lambda t, pt: (t // n_chunks, 0, 0))
    hbm = pl.BlockSpec(memory_space=pl.ANY)
    grid_spec = pltpu.PrefetchScalarGridSpec(
        num_scalar_prefetch=1, grid=(b * n_chunks,),
        in_specs=[seq(HEADS, 2 * LANES), seq(HEADS, LANES), seq(1, 2 * LANES), seq(1, LANES), seq(1, LANES),
                  hbm, hbm, hbm, hbm],
        out_specs=[seq(HEADS, LANES), seq(HEADS, LANES)],
        scratch_shapes=[pltpu.VMEM((2, keys, LANES), F32), pltpu.VMEM((2, c_krt.shape[2], keys), F32),
                        pltpu.VMEM((2, LANES, keys), F32), pltpu.VMEM((2, LANES, keys), F32),
                        pltpu.SemaphoreType.DMA((2,)),
                        pltpu.VMEM((2, HEADS, 1), F32), pltpu.VMEM((2, HEADS, 1), F32),
                        pltpu.VMEM((2, HEADS, LANES), F32)])
    kern = functools.partial(_decode_kernel, layer=layer, n_chunks=n_chunks, cp=cp, page=page)
    return pl.pallas_call(
        kern, grid_spec=grid_spec,
        out_shape=[jax.ShapeDtypeStruct((b, HEADS, LANES), F32), jax.ShapeDtypeStruct((b, HEADS, LANES), F32)],
        compiler_params=_params(("arbitrary",)), name="decode",
    )(page_table, qm, qd, kn_m, kn_d, vn_d, c_lat, c_krt, c_dkt, c_dvt)


def _sample_epilogue_kernel(om_ref, od_ref, wuv_ref, lam_ref, subln_ref, omla_ref, odiff_ref, *, lam_init):
    om = [om_ref[h] for h in range(HEADS)]
    od = [od_ref[h] for h in range(HEADS)]
    omla_ref[...] = _mla_epilogue(om, wuv_ref).astype(BF16)
    chunks = _diff_epilogue(od, _diff_lambda(lam_ref, lam_init), subln_ref[...], lam_init)
    odiff_ref[:, 0:LANES] = chunks[0].astype(BF16)
    odiff_ref[:, LANES:2 * LANES] = chunks[1].astype(BF16)


def _sample_epilogue(om, od, w, lam_init):
    t = om.shape[1]
    return pl.pallas_call(
        functools.partial(_sample_epilogue_kernel, lam_init=lam_init),
        out_shape=[jax.ShapeDtypeStruct((t, 512), BF16), jax.ShapeDtypeStruct((t, 2 * LANES), BF16)],
        name="sample_epilogue",
    )(om, od, w["wuv_pad"], w["lam"], w["subln"])


def _outproj_kernel(x_ref, a_ref, b_ref, c_ref, w_ref, g_ref, xo_ref, h_ref):
    na, nb = a_ref.shape[1], b_ref.shape[1]
    y = (jnp.dot(a_ref[...], w_ref[0:na, :], preferred_element_type=F32)
         + jnp.dot(b_ref[...], w_ref[na:na + nb, :], preferred_element_type=F32)
         + jnp.dot(c_ref[...], w_ref[na + nb:, :], preferred_element_type=F32))
    xo = x_ref[...] + y
    xo_ref[...] = xo
    h_ref[...] = _rms(xo, g_ref[...], NORM_EPS).astype(BF16)


def _outproj(x, a, b, c, w_out, g, *, tm):
    t, d = x.shape
    tm = min(tm, t)
    tok = lambda n: pl.BlockSpec((tm, n), lambda i: (i, 0))
    return pl.pallas_call(
        _outproj_kernel, grid=(t // tm,),
        in_specs=[tok(d), tok(a.shape[1]), tok(b.shape[1]), tok(c.shape[1]), _full(w_out.shape), _full((1, d))],
        out_specs=[tok(d), tok(d)],
        out_shape=[jax.ShapeDtypeStruct((t, d), F32), jax.ShapeDtypeStruct((t, d), BF16)],
        compiler_params=_params(("parallel",)), name="outproj",
    )(x, a, b, c, w_out, g)


def _swiglu_tile(h, wg, wu, wd):
    g = jnp.dot(h, wg, preferred_element_type=F32)
    u = jnp.dot(h, wu, preferred_element_type=F32)
    act = (g * jax.nn.sigmoid(g) * u).astype(BF16)
    return jnp.dot(act, wd, preferred_element_type=F32)


def _ffn_kernel(x_ref, h_ref, wg_ref, wu_ref, wd_ref, gf_ref, o_ref, acc_ref, *, final_norm):
    f = pl.program_id(1)

    @pl.when(f == 0)
    def _():
        acc_ref[...] = jnp.zeros(acc_ref.shape, F32)

    acc_ref[...] += _swiglu_tile(h_ref[...], wg_ref[...], wu_ref[...], wd_ref[...])

    @pl.when(f == pl.num_programs(1) - 1)
    def _():
        y = x_ref[...] + acc_ref[...]
        o_ref[...] = _rms(y, gf_ref[...], NORM_EPS) if final_norm else y


def _ffn(x, h, wg, wu, wd, g_final, *, final_norm, tm, tf):
    t, d = x.shape
    dff = wg.shape[1]
    tm = min(tm, t)
    assert dff % tf == 0
    return pl.pallas_call(
        functools.partial(_ffn_kernel, final_norm=final_norm), grid=(t // tm, dff // tf),
        in_specs=[pl.BlockSpec((tm, d), lambda i, f: (i, 0)), pl.BlockSpec((tm, d), lambda i, f: (i, 0)),
                  pl.BlockSpec((d, tf), lambda i, f: (0, f)), pl.BlockSpec((d, tf), lambda i, f: (0, f)),
                  pl.BlockSpec((tf, d), lambda i, f: (f, 0)), pl.BlockSpec((1, d), lambda i, f: (0, 0))],
        out_specs=pl.BlockSpec((tm, d), lambda i, f: (i, 0)),
        out_shape=jax.ShapeDtypeStruct((t, d), F32),
        scratch_shapes=[pltpu.VMEM((tm, d), F32)],
        compiler_params=_params(("parallel", "arbitrary")), name="ffn",
    )(x, h, wg, wu, wd, g_final)


def _router_kernel(x_ref, g_ref, wr_ref, gate_ref, *, n_experts):
    h = _rms(x_ref[...], g_ref[...], NORM_EPS)
    logits = jnp.dot(h, wr_ref[...], precision=lax.Precision.HIGHEST, preferred_element_type=F32)
    lane = lax.broadcasted_iota(jnp.int32, logits.shape, 1).astype(F32)
    logits = jnp.where(lane < n_experts, logits, NEG_INF)
    gate = jnp.zeros(logits.shape, F32)
    tops, picks = [], []
    for _ in range(TOP_K):
        top = jnp.max(logits, axis=-1, keepdims=True)
        pick = jnp.min(jnp.where(logits == top, lane, float(LANES)), axis=-1, keepdims=True)
        tops.append(top)
        picks.append(pick)
        logits = jnp.where(lane == pick, NEG_INF, logits)
    e = [jnp.exp(tv - tops[0]) for tv in tops]
    den = e[0]
    for ev in e[1:]:
        den = den + ev
    for ev, pick in zip(e, picks):
        gate = gate + jnp.where(lane == pick, ev / den, 0.0)
    gate_ref[...] = gate


def _router(x, g, wr_pad, n_experts, *, tm):
    t, d = x.shape
    tm = min(tm, t)
    return pl.pallas_call(
        functools.partial(_router_kernel, n_experts=n_experts), grid=(t // tm,),
        in_specs=[pl.BlockSpec((tm, d), lambda i: (i, 0)), _full((1, d)), _full(wr_pad.shape)],
        out_specs=pl.BlockSpec((tm, LANES), lambda i: (i, 0)),
        out_shape=jax.ShapeDtypeStruct((t, LANES), F32),
        compiler_params=_params(("parallel",)), name="router",
    )(x, g, wr_pad)


def _moe_kernel(x_ref, h_ref, gate_ref, wg_ref, wu_ref, wd_ref, gf_ref, o_ref, acc_ref, *, final_norm):
    e = pl.program_id(1)
    f = pl.program_id(2)

    @pl.when((e == 0) & (f == 0))
    def _():
        acc_ref[...] = jnp.zeros(acc_ref.shape, F32)

    lane = lax.broadcasted_iota(jnp.int32, (1, LANES), 1)
    ge = jnp.sum(jnp.where(lane == e, gate_ref[...], 0.0), axis=-1, keepdims=True)
    acc_ref[...] += ge * _swiglu_tile(h_ref[...], wg_ref[...], wu_ref[...], wd_ref[...])

    @pl.when((e == pl.num_programs(1) - 1) & (f == pl.num_programs(2) - 1))
    def _():
        y = x_ref[...] + acc_ref[...]
        o_ref[...] = _rms(y, gf_ref[...], NORM_EPS) if final_norm else y


def _moe(x, h, gate, wg, wu, wd, g_final, *, final_norm, tm, tf):
    t, d = x.shape
    n_e, _, dff = wg.shape
    tm = min(tm, t)
    assert dff % tf == 0
    return pl.pallas_call(
        functools.partial(_moe_kernel, final_norm=final_norm), grid=(t // tm, n_e, dff // tf),
        in_specs=[pl.BlockSpec((tm, d), lambda i, e, f: (i, 0)), pl.BlockSpec((tm, d), lambda i, e, f: (i, 0)),
                  pl.BlockSpec((tm, LANES), lambda i, e, f: (i, 0)),
                  pl.BlockSpec((None, d, tf), lambda i, e, f: (e, 0, f)),
                  pl.BlockSpec((None, d, tf), lambda i, e, f: (e, 0, f)),
                  pl.BlockSpec((None, tf, d), lambda i, e, f: (e, f, 0)),
                  pl.BlockSpec((1, d), lambda i, e, f: (0, 0))],
        out_specs=pl.BlockSpec((tm, d), lambda i, e, f: (i, 0)),
        out_shape=jax.ShapeDtypeStruct((t, d), F32),
        scratch_shapes=[pltpu.VMEM((tm, d), F32)],
        compiler_params=_params(("parallel", "arbitrary", "arbitrary")), name="moe",
    )(x, h, gate, wg, wu, wd, g_final)


def _rot_cols(w):
    return jnp.concatenate([-w[..., ROPE_HALF:], w[..., :ROPE_HALF]], axis=-1)


def _pad_cols(w, n):
    return jnp.pad(w, ((0, 0), (0, n - w.shape[1])))


def _layer_weights(l, P):
    d = P["w_in"].shape[1]
    cq_n = P["mla_q_norm"].shape[1]
    ckv_n = P["mla_kv_norm"].shape[1]
    heads, nope = P["mla_w_uk"].shape[2], P["mla_w_uk"].shape[3]
    rope_n = P["mla_w_uq"].shape[3] - nope
    v_n = P["mla_w_uv"].shape[3]
    dh = P["diff_lambda_q1"].shape[1]
    n_groups, chunk = P["chunk_w_s"].shape[1], P["chunk_w_s"].shape[2]
    cw = P["chunk_ln_g"].shape[1]
    assert (cq_n, ckv_n, heads, rope_n, dh, n_groups, chunk, cw) == (256, 128, HEADS, 32, 32, 4, CHUNK, 256)
    dq_n, dk_n, dv_n = 256, 128, 128
    assert P["w_in"].shape[2] == cq_n + ckv_n + rope_n + dq_n + dk_n + dv_n + 2 * cw
    w_in = P["w_in"][l]
    o = 0
    parts = {}
    for name, n in (("cq", cq_n), ("ckv", ckv_n), ("kr", rope_n), ("dq", dq_n), ("dk", dk_n), ("dv", dv_n),
                    ("cu", cw), ("cv", cw)):
        parts[name] = w_in[:, o:o + n]
        o += n
    rot32 = lambda w: _rot_cols(w.reshape(d, -1, 2 * ROPE_HALF)).reshape(d, -1)
    win = jnp.concatenate([
        parts["cq"], parts["ckv"], _pad_cols(parts["kr"], LANES), _pad_cols(rot32(parts["kr"]), LANES),
        parts["dq"], rot32(parts["dq"]), parts["dk"], rot32(parts["dk"]), parts["dv"], parts["cu"], parts["cv"],
    ], axis=1).astype(BF16)

    uq = jnp.transpose(P["mla_w_uq"][l], (1, 0, 2))
    uk = jnp.transpose(P["mla_w_uk"][l], (1, 0, 2))
    wq_lat = _fold_q(uq[:, :, :nope], uk)
    uq_r = uq[:, :, nope:]
    pad_r = lambda w: jnp.transpose(jnp.pad(w, ((0, 0), (0, 0), (0, LANES - rope_n))), (1, 0, 2)).reshape(cq_n, -1)
    wq = jnp.concatenate([wq_lat, pad_r(uq_r), pad_r(_rot_cols(uq_r))], axis=1).astype(BF16)

    uv = jnp.transpose(P["mla_w_uv"][l], (1, 0, 2))
    wuv_pad = jnp.stack([jnp.pad(uv[h], ((0, 0), (h * v_n, (heads - 1 - h) * v_n))) for h in range(heads)])

    ws = P["chunk_w_s"][l]
    bs = P["chunk_b_s"][l]
    rep = lambda a: jnp.repeat(a, cw // n_groups, axis=-1)
    return dict(
        win=win, wq=wq, wuv_pad=wuv_pad.astype(BF16),
        gmix=P["norm_mix"][l][None], gq=P["mla_q_norm"][l][None], gkv=P["mla_kv_norm"][l][None],
        lng=P["chunk_ln_g"][l][None], lnb=P["chunk_ln_b"][l][None],
        ws=ws, bs_full=rep(jnp.transpose(bs)),
        ws_row0=rep(ws[:, 0, 0][None]), bs_row0=rep(bs[:, 0][None]),
        lam=jnp.stack([P["diff_lambda_q1"][l], P["diff_lambda_k1"][l], P["diff_lambda_q2"][l],
                       P["diff_lambda_k2"][l]]),
        subln=jnp.tile(P["diff_subln"][l], 2)[None],
        w_out=P["w_out"][l].astype(BF16), gffn=P["norm_ffn"][l][None],
        mla_scale=float((nope + rope_n) ** -0.5), diff_scale=float(dh ** -0.5),
    )


def _rope_tables(pos):
    inv = 1.0 / (ROPE_THETA ** (jnp.arange(ROPE_HALF, dtype=F32) / ROPE_HALF))
    ang = pos.astype(F32)[:, None] * inv[None, :]
    reps = LANES // ROPE_HALF
    return jnp.tile(jnp.cos(ang), (1, reps)), jnp.tile(jnp.sin(ang), (1, reps))


def kernel(x_prompt, x_sample, cache_mla_latent, cache_mla_krope, cache_diff_k, cache_diff_v, page_table, norm_mix, w_in, mla_q_norm, mla_w_uq, mla_kv_norm, mla_w_uk, mla_w_uv, diff_lambda_q1, diff_lambda_k1, diff_lambda_q2, diff_lambda_k2, diff_subln, chunk_ln_g, chunk_ln_b, chunk_w_s, chunk_b_s, w_out, norm_ffn, ffn_w_gate, ffn_w_up, ffn_w_down, moe_w_router, moe_w_gate, moe_w_up, moe_w_down, norm_final):
    P = dict(norm_mix=norm_mix, w_in=w_in, mla_q_norm=mla_q_norm, mla_w_uq=mla_w_uq, mla_kv_norm=mla_kv_norm,
             mla_w_uk=mla_w_uk, mla_w_uv=mla_w_uv, diff_lambda_q1=diff_lambda_q1, diff_lambda_k1=diff_lambda_k1,
             diff_lambda_q2=diff_lambda_q2, diff_lambda_k2=diff_lambda_k2, diff_subln=diff_subln,
             chunk_ln_g=chunk_ln_g, chunk_ln_b=chunk_ln_b, chunk_w_s=chunk_w_s, chunk_b_s=chunk_b_s,
             w_out=w_out, norm_ffn=norm_ffn)
    depth = w_in.shape[0]
    bp, s, d = x_prompt.shape
    bs, ts, _ = x_sample.shape
    assert bp == 1 and ts == 1
    n_pool, page = cache_mla_latent.shape[1], cache_mla_latent.shape[2]
    past_len = page_table.shape[1] * page
    assert past_len % CHUNK == 0
    n_experts = moe_w_router.shape[2]

    cos_p, sin_p = _rope_tables(jnp.arange(s))
    cos_s, sin_s = _rope_tables(jnp.full((bs,), past_len))
    c_krt = jnp.transpose(cache_mla_krope, (0, 1, 3, 2))
    c_dkt = jnp.transpose(cache_diff_k, (0, 1, 3, 4, 5, 2)).reshape(depth, n_pool, LANES, page)
    c_dvt = jnp.transpose(cache_diff_v, (0, 1, 3, 4, 2)).reshape(depth, n_pool, LANES, page)
    gfin = norm_final[None]

    xp = x_prompt.reshape(s, d)
    xs = x_sample.reshape(bs, d)
    st_p, st_s = [], []
    for l in range(depth):
        w = _layer_weights(l, P)
        lam_init = 0.8 - 0.6 * math.exp(-0.3 * l)
        last = l == depth - 1
        j = l // 2
        if l % 2 == 0:
            mix_w = (ffn_w_gate[j].astype(BF16), ffn_w_up[j].astype(BF16), ffn_w_down[j].astype(BF16))
        else:
            mix_w = (moe_w_gate[j].astype(BF16), moe_w_up[j].astype(BF16), moe_w_down[j].astype(BF16))
            wr_pad = _pad_cols(moe_w_router[j], LANES)

        def channel_mix(x, h, tm):
            if l % 2 == 0:
                return _ffn(x, h, *mix_w, gfin, final_norm=last, tm=tm, tf=mix_w[0].shape[1] // 2)
            gate = _router(x, w["gffn"], wr_pad, n_experts, tm=tm)
            return _moe(x, h, gate, *mix_w, gfin, final_norm=last, tm=tm, tf=mix_w[0].shape[2] // 2)

        qm, kc, qd, kd, vd, oc, ckv, kr, dk, dv, vst = _prep(xs, cos_s, sin_s, w, sample_mode=True, tm=128)
        om, od = _decode(l, page_table, jnp.transpose(qm, (1, 0, 2)), jnp.transpose(qd, (1, 0, 2)),
                         kc[:, None], kd[:, None], vd[:, None],
                         cache_mla_latent, c_krt, c_dkt, c_dvt, cp=32)
        o_mla, o_diff = _sample_epilogue(jnp.transpose(om, (1, 0, 2)), jnp.transpose(od, (1, 0, 2)), w, lam_init)
        xs, hs = _outproj(xs, o_mla, o_diff, oc, w["w_out"], w["gffn"], tm=128)
        xs = channel_mix(xs, hs, 128)
        st_s.append((ckv, kr, dk, dv, vst))

        qm, kc, qd, kd, _, oc, ckv, kr, dk, dv, vme, vde = _prep(xp, cos_p, sin_p, w, sample_mode=False, tm=256)
        o_mla = _flash(qm, kc, vme, (w["wuv_pad"],), mode="mla", lam_init=lam_init, tq=512, tk=1024, groups=8)
        o_diff = _flash(qd, kd, vde, (w["lam"], w["subln"]), mode="diff", lam_init=lam_init, tq=512, tk=2048,
                        groups=8)
        xp, hp = _outproj(xp, o_mla, o_diff, oc, w["w_out"], w["gffn"], tm=512)
        xp = channel_mix(xp, hp, 512)
        st_p.append((ckv, kr, dk, dv))

    stack = lambda sts, i, shape: jnp.stack([st[i] for st in sts], axis=0).reshape(shape)
    return (xp.reshape(bp, s, d), xs.reshape(bs, ts, d),
            stack(st_p, 0, (depth, bp, s, -1)), stack(st_p, 1, (depth, bp, s, -1)),
            stack(st_p, 2, (depth, bp, s) + cache_diff_k.shape[3:]),
            stack(st_p, 3, (depth, bp, s) + cache_diff_v.shape[3:]),
            stack(st_s, 0, (depth, bs, ts, -1)), stack(st_s, 1, (depth, bs, ts, -1)),
            stack(st_s, 2, (depth, bs, ts) + cache_diff_k.shape[3:]),
            stack(st_s, 3, (depth, bs, ts) + cache_diff_v.shape[3:]),
            stack(st_s, 4, (depth, bs, ts, -1)))
```

```python
import functools
import math

import jax
import jax.numpy as jnp
from jax import lax
from jax.experimental import pallas as pl
from jax.experimental.pallas import tpu as pltpu

F32 = jnp.float32
BF16 = jnp.bfloat16

ROPE_THETA = 10000.0
NORM_EPS = 1e-6
LN_EPS = 1e-5
NEG_INF = -1e30
TOP_K = 2
LOG2E = math.log2(math.e)

LANES = 128
HEADS = 8
ROPE_HALF = 16
CHUNK = 128
VMEM_LIMIT = 56 * 1024 * 1024


def _params(sem):
    return pltpu.CompilerParams(dimension_semantics=sem, vmem_limit_bytes=VMEM_LIMIT)


def _rms(x, g, eps):
    return x * lax.rsqrt(jnp.mean(x * x, axis=-1, keepdims=True) + eps) * g


def _gelu(x):
    return 0.5 * x * (1.0 + jnp.tanh(math.sqrt(2.0 / math.pi) * (x + 0.044715 * (x * x * x))))


def _full(shape):
    return pl.BlockSpec(shape, lambda *_: (0,) * len(shape))


def _fold_q_kernel(uq_ref, uk_ref, o_ref):
    for h in range(HEADS):
        o_ref[:, h * LANES:(h + 1) * LANES] = lax.dot_general(
            uq_ref[h], uk_ref[h], (((1,), (1,)), ((), ())),
            precision=lax.Precision.HIGHEST, preferred_element_type=F32)


def _fold_q(uq_nope, uk):
    h, cq, _ = uq_nope.shape
    return pl.pallas_call(
        _fold_q_kernel,
        out_shape=jax.ShapeDtypeStruct((cq, h * uk.shape[1]), F32),
        name="fold_q",
    )(uq_nope, uk)


def _prep_kernel(x_ref, gmix_ref, win_ref, gq_ref, wq_ref, gkv_ref, cos_ref, sin_ref, lng_ref, lnb_ref,
                 ws_ref, bs_ref,
                 qm_ref, kc_ref, qd_ref, kd_ref, vd_ref, oc_ref, ckv_ref, kr_ref, dk_ref, dv_ref, *rest,
                 sample_mode, mla_scale, diff_scale):
    tm = x_ref.shape[0]
    x = x_ref[...]
    xn = _rms(x, gmix_ref[...], NORM_EPS).astype(BF16)
    z = jnp.dot(xn, win_ref[...], preferred_element_type=F32)
    cos = cos_ref[...]
    sin = sin_ref[...]
    col = lambda a, n=1: z[:, a * LANES:(a + n) * LANES]

    cqn = _rms(col(0, 2), gq_ref[...], NORM_EPS).astype(BF16)
    qall = jnp.dot(cqn, wq_ref[...], preferred_element_type=F32)
    for h in range(HEADS):
        qlat = qall[:, h * LANES:(h + 1) * LANES]
        qa = qall[:, (HEADS + h) * LANES:(HEADS + h + 1) * LANES]
        qb = qall[:, (2 * HEADS + h) * LANES:(2 * HEADS + h + 1) * LANES]
        qm_ref[h, :, 0:LANES] = (qlat * mla_scale).astype(BF16)
        qm_ref[h, :, LANES:2 * LANES] = ((qa * cos + qb * sin) * mla_scale).astype(BF16)
    ckvn = _rms(col(2), gkv_ref[...], NORM_EPS)
    kr = col(3) * cos + col(4) * sin
    ckv_ref[...] = ckvn
    kr_ref[...] = kr[:, 0:2 * ROPE_HALF]
    kc_ref[:, 0:LANES] = ckvn.astype(BF16)
    kc_ref[:, LANES:2 * LANES] = kr.astype(BF16)

    lane = lax.broadcasted_iota(jnp.int32, (1, LANES), 1)
    for g in range(2):
        chunk = (col(5 + g) * cos + col(7 + g) * sin) * diff_scale
        rolled = pltpu.roll(chunk, LANES // 2, axis=1)
        for r in range(2):
            src = chunk if r == g else rolled
            for m in range(2):
                lo = g * 64 + m * 32
                sel = (lane >= lo) & (lane < lo + 32)
                qd_ref[g * 4 + r * 2 + m] = jnp.where(sel, src, 0.0).astype(BF16)
    dk = col(9) * cos + col(10) * sin
    dv = col(11)
    dk_ref[...] = dk
    dv_ref[...] = dv
    kd_ref[...] = dk.astype(BF16)
    vd_ref[...] = dv.astype(BF16)

    u = _gelu(col(12, 2))
    gv = _gelu(col(14, 2))
    mu = jnp.mean(gv, axis=-1, keepdims=True)
    gc = gv - mu
    var = jnp.mean(gc * gc, axis=-1, keepdims=True)
    v = gc * lax.rsqrt(var + LN_EPS) * lng_ref[...] + lnb_ref[...]
    if sample_mode:
        rest[0][...] = v
        oc_ref[...] = (u * (v * ws_ref[...] + bs_ref[...])).astype(BF16)
    else:
        vme_ref, vde_ref = rest
        ones_col = jnp.broadcast_to(jnp.where(lane == 0, 1.0, 0.0), (tm, LANES)).astype(BF16)
        vme_ref[:, 0:LANES] = ckvn.astype(BF16)
        vme_ref[:, LANES:2 * LANES] = ones_col
        vde_ref[:, 0:LANES] = dv.astype(BF16)
        vde_ref[:, LANES:2 * LANES] = ones_col
        lane2 = lax.broadcasted_iota(jnp.int32, (1, 2 * LANES), 1)
        row = lax.broadcasted_iota(jnp.int32, (CHUNK, CHUNK), 0)
        cl = lax.broadcasted_iota(jnp.int32, (CHUNK, CHUNK), 1)
        wts = [jnp.where(row >= cl, ws_ref[g], 0.0).astype(BF16) for g in range(4)]
        for c in range(tm // CHUNK):
            vc = v[c * CHUNK:(c + 1) * CHUNK]
            mix = bs_ref[...]
            for g in range(4):
                vg = jnp.where((lane2 >= g * 64) & (lane2 < (g + 1) * 64), vc, 0.0).astype(BF16)
                mix = mix + jnp.dot(wts[g], vg, preferred_element_type=F32)
            oc_ref[c * CHUNK:(c + 1) * CHUNK, :] = (u[c * CHUNK:(c + 1) * CHUNK] * mix).astype(BF16)


def _prep(x, cos, sin, w, *, sample_mode, tm):
    t, d = x.shape
    tm = min(tm, t)
    nin = w["win"].shape[1]
    nq = w["wq"].shape[1]
    tok = lambda n: pl.BlockSpec((tm, n), lambda i: (i, 0))
    heads = lambda n: pl.BlockSpec((HEADS, tm, n), lambda i: (0, i, 0))
    ws, bs = (w["ws_row0"], w["bs_row0"]) if sample_mode else (w["ws"], w["bs_full"])
    in_specs = [tok(d), _full((1, d)), _full((d, nin)), _full((1, 2 * LANES)), _full((2 * LANES, nq)),
                _full((1, LANES)), tok(LANES), tok(LANES), _full((1, 2 * LANES)), _full((1, 2 * LANES)),
                _full(ws.shape), _full(bs.shape)]
    out_shape = [jax.ShapeDtypeStruct((HEADS, t, 2 * LANES), BF16),
                 jax.ShapeDtypeStruct((t, 2 * LANES), BF16),
                 jax.ShapeDtypeStruct((HEADS, t, LANES), BF16),
                 jax.ShapeDtypeStruct((t, LANES), BF16),
                 jax.ShapeDtypeStruct((t, LANES), BF16),
                 jax.ShapeDtypeStruct((t, 2 * LANES), BF16),
                 jax.ShapeDtypeStruct((t, LANES), F32),
                 jax.ShapeDtypeStruct((t, 2 * ROPE_HALF), F32),
                 jax.ShapeDtypeStruct((t, LANES), F32),
                 jax.ShapeDtypeStruct((t, LANES), F32)]
    out_specs = [heads(2 * LANES), tok(2 * LANES), heads(LANES), tok(LANES), tok(LANES), tok(2 * LANES),
                 tok(LANES), tok(2 * ROPE_HALF), tok(LANES), tok(LANES)]
    if sample_mode:
        out_shape.append(jax.ShapeDtypeStruct((t, 2 * LANES), F32))
        out_specs.append(tok(2 * LANES))
    else:
        out_shape += [jax.ShapeDtypeStruct((t, 2 * LANES), BF16)] * 2
        out_specs += [tok(2 * LANES)] * 2
    kern = functools.partial(_prep_kernel, sample_mode=sample_mode, mla_scale=w["mla_scale"] * LOG2E,
                             diff_scale=w["diff_scale"] * LOG2E)
    return pl.pallas_call(
        kern, grid=(t // tm,), in_specs=in_specs, out_specs=out_specs, out_shape=out_shape,
        compiler_params=_params(("parallel",)), name="prep_sample" if sample_mode else "prep_prompt",
    )(x, w["gmix"], w["win"], w["gq"], w["wq"], w["gkv"], cos, sin, w["lng"], w["lnb"], ws, bs)


def _mla_epilogue(o, wuv_ref):
    out = None
    for h in range(HEADS):
        part = jnp.dot(o[h].astype(BF16), wuv_ref[h], preferred_element_type=F32)
        out = part if out is None else out + part
    return out


def _diff_lambda(lam_ref, lam_init):
    p = lam_ref[...]
    s1 = jnp.sum(p[0:1] * p[1:2], axis=-1, keepdims=True)
    s2 = jnp.sum(p[2:3] * p[3:4], axis=-1, keepdims=True)
    return jnp.exp(s1) - jnp.exp(s2) + lam_init


def _diff_epilogue(o, lam, subln, lam_init):
    lane = lax.broadcasted_iota(jnp.int32, (1, LANES), 1)
    chunks = []
    for g in range(2):
        valid = (lane >= g * 64) & (lane < (g + 1) * 64)
        halves = []
        for r in range(2):
            d = o[g * 4 + r * 2] - lam * o[g * 4 + r * 2 + 1]
            ms = jnp.sum(jnp.where(valid, d * d, 0.0), axis=-1, keepdims=True) * (1.0 / 64.0)
            y = d * lax.rsqrt(ms + LN_EPS) * subln * (1.0 - lam_init)
            halves.append(y if r == g else pltpu.roll(y, LANES // 2, axis=1))
        chunks.append(jnp.where(lane < 64, halves[0], halves[1]))
    return chunks


def _flash_kernel(q_ref, kt_ref, v_ref, *rest, tq, tk, groups, mode, lam_init):
    if mode == "mla":
        wuv_ref, out_ref, m_ref, acc_ref = rest
    else:
        lam_ref, subln_ref, out_ref, m_ref, acc_ref = rest
    i = pl.program_id(0)
    hg = HEADS // groups
    rows = hg * tq
    m_ref[...] = jnp.full(m_ref.shape, NEG_INF, F32)
    acc_ref[...] = jnp.zeros(acc_ref.shape, F32)

    def step(j, masked):
        kt = kt_ref[j]
        v = v_ref[pl.ds(pl.multiple_of(j * tk, tk), tk), :]
        if masked:
            kpos = j * tk + lax.broadcasted_iota(jnp.int32, (1, tk), 1)
            qpos = i * tq + (lax.broadcasted_iota(jnp.int32, (rows, 1), 0) & (tq - 1))
            keep = kpos <= qpos
        for g in range(groups):
            q = q_ref[g * hg:(g + 1) * hg].reshape(rows, q_ref.shape[2])
            s = jnp.dot(q, kt, preferred_element_type=F32)
            if masked:
                s = jnp.where(keep, s, NEG_INF)
            m_prev = m_ref[g]
            m_new = jnp.maximum(m_prev, jnp.max(s, axis=-1, keepdims=True))
            alpha = jnp.exp2(m_prev - m_new)
            p = jnp.exp2(s - m_new).astype(BF16)
            acc_ref[g] = alpha * acc_ref[g] + jnp.dot(p, v, preferred_element_type=F32)
            m_ref[g] = m_new

    n_full = (i * tq) // tk

    def body(j, carry):
        step(j, False)
        return carry

    lax.fori_loop(0, n_full, body, 0)
    step(n_full, True)

    o = []
    for g in range(groups):
        acc = acc_ref[g]
        og = acc[:, 0:LANES] / acc[:, LANES:LANES + 1]
        o += [og[h * tq:(h + 1) * tq] for h in range(hg)]
    if mode == "mla":
        out_ref[...] = _mla_epilogue(o, wuv_ref).astype(BF16)
    else:
        chunks = _diff_epilogue(o, _diff_lambda(lam_ref, lam_init), subln_ref[...], lam_init)
        out_ref[:, 0:LANES] = chunks[0].astype(BF16)
        out_ref[:, LANES:2 * LANES] = chunks[1].astype(BF16)


def _flash(q, k, v_ext, extra, *, mode, lam_init, tq, tk, groups):
    _, s, dk = q.shape
    tq, tk = min(tq, s), min(tk, s)
    assert tk % tq == 0 and s % tk == 0 and tq & (tq - 1) == 0 and HEADS % groups == 0
    kt = jnp.transpose(k.reshape(s // tk, tk, dk), (0, 2, 1))
    in_specs = [pl.BlockSpec((HEADS, tq, dk), lambda i: (0, i, 0)), _full(kt.shape), _full(v_ext.shape)]
    in_specs += [_full(e.shape) for e in extra]
    nout = 512 if mode == "mla" else 2 * LANES
    rows = HEADS // groups * tq
    kern = functools.partial(_flash_kernel, tq=tq, tk=tk, groups=groups, mode=mode, lam_init=lam_init)
    return pl.pallas_call(
        kern, grid=(s // tq,), in_specs=in_specs,
        out_specs=pl.BlockSpec((tq, nout), lambda i: (i, 0)),
        out_shape=jax.ShapeDtypeStruct((s, nout), BF16),
        scratch_shapes=[pltpu.VMEM((groups, rows, 1), F32), pltpu.VMEM((groups, rows, 2 * LANES), F32)],
        compiler_params=_params(("parallel",)), name="flash_" + mode,
    )(q, kt, v_ext, *extra)


def _decode_kernel(pt_ref, qm_ref, qd_ref, knm_ref, knd_ref, vnd_ref, lat_hbm, kr_hbm, dk_hbm, dv_hbm,
                   om_ref, od_ref, lat_buf, kr_buf, dk_buf, dv_buf, sem, m_ref, l_ref, acc_ref,
                   *, layer, n_chunks, cp, page):
    t = pl.program_id(0)
    c = t % n_chunks
    slot = t % 2

    def chunk_copies(step, slot):
        b = step // n_chunks
        first = (step % n_chunks) * cp
        out = []
        for i in range(cp):
            pg = pt_ref[b, first + i]
            tok = pl.ds(i * page, page)
            out.append(pltpu.make_async_copy(lat_hbm.at[layer, pg], lat_buf.at[slot, tok, :], sem.at[slot]))
            out.append(pltpu.make_async_copy(kr_hbm.at[layer, pg], kr_buf.at[slot, :, tok], sem.at[slot]))
            out.append(pltpu.make_async_copy(dk_hbm.at[layer, pg], dk_buf.at[slot, :, tok], sem.at[slot]))
            out.append(pltpu.make_async_copy(dv_hbm.at[layer, pg], dv_buf.at[slot, :, tok], sem.at[slot]))
        return out

    @pl.when(t == 0)
    def _():
        for cpy in chunk_copies(0, 0):
            cpy.start()

    @pl.when(t + 1 < pl.num_programs(0))
    def _():
        for cpy in chunk_copies(t + 1, 1 - slot):
            cpy.start()

    for cpy in chunk_copies(t, slot):
        cpy.wait()

    @pl.when(c == 0)
    def _():
        m_ref[...] = jnp.full(m_ref.shape, NEG_INF, F32)
        l_ref[...] = jnp.zeros(l_ref.shape, F32)
        acc_ref[...] = jnp.zeros(acc_ref.shape, F32)

    nt = (((1,), (1,)), ((), ()))
    qm = qm_ref[...]
    qd = qd_ref[...]
    lat = lat_buf[slot].astype(BF16)
    s_m = (lax.dot_general(qm[:, 0:LANES], lat, nt, preferred_element_type=F32)
           + jnp.dot(qm[:, LANES:LANES + 2 * ROPE_HALF], kr_buf[slot].astype(BF16), preferred_element_type=F32))
    s_d = jnp.dot(qd, dk_buf[slot].astype(BF16), preferred_element_type=F32)

    def update(a, s, pv):
        m_prev = m_ref[a]
        m_new = jnp.maximum(m_prev, jnp.max(s, axis=-1, keepdims=True))
        alpha = jnp.exp2(m_prev - m_new)
        pr = jnp.exp2(s - m_new)
        l_ref[a] = alpha * l_ref[a] + jnp.sum(pr, axis=-1, keepdims=True)
        acc_ref[a] = alpha * acc_ref[a] + pv(pr.astype(BF16))
        m_ref[a] = m_new

    update(0, s_m, lambda pr: jnp.dot(pr, lat, preferred_element_type=F32))
    update(1, s_d, lambda pr: lax.dot_general(pr, dv_buf[slot].astype(BF16), nt, preferred_element_type=F32))

    @pl.when(c == n_chunks - 1)
    def _():
        def finish(a, q, kn, vn, o_ref):
            s = jnp.sum(q.astype(F32) * kn.astype(F32), axis=-1, keepdims=True)
            m_prev = m_ref[a]
            m_new = jnp.maximum(m_prev, s)
            alpha = jnp.exp2(m_prev - m_new)
            pr = jnp.exp2(s - m_new)
            l = alpha * l_ref[a] + pr
            acc = alpha * acc_ref[a] + pr * vn.astype(F32)
            o_ref[...] = acc / l

        finish(0, qm, knm_ref[...], knm_ref[:, 0:LANES], om_ref)
        finish(1, qd, knd_ref[...], vnd_ref[...], od_ref)


def _decode(layer, page_table, qm, qd, kn_m, kn_d, vn_d, c_lat, c_krt, c_dkt, c_dvt, *, cp):
    b, n_pages = page_table.shape
    page = c_lat.shape[2]
    cp = min(cp, n_pages)
    assert n_pages % cp == 0
    n_chunks = n_pages // cp
    keys = cp * page
    seq = lambda n, w: pl.BlockSpec((None, n, w), lambda t, pt: (t // n_chunks, 0, 0))
    hbm = pl.BlockSpec(memory_space=pl.ANY)
    grid_spec = pltpu.PrefetchScalarGridSpec(
        num_scalar_prefetch=1, grid=(b * n_chunks,),
        in_specs=[seq(HEADS, 2 * LANES), seq(HEADS, LANES), seq(1, 2 * LANES), seq(1, LANES), seq(1, LANES),
                  hbm, hbm, hbm, hbm],
        out_specs=[seq(HEADS, LANES), seq(HEADS, LANES)],
        scratch_shapes=[pltpu.VMEM((2, keys, LANES), F32), pltpu.VMEM((2, c_krt.shape[2], keys), F32),
                        pltpu.VMEM((2, LANES, keys), F32), pltpu.VMEM((2, LANES, keys), F32),
                        pltpu.SemaphoreType.DMA((2,)),
                        pltpu.VMEM((2, HEADS, 1), F32), pltpu.VMEM((2, HEADS, 1), F32),
                        pltpu.VMEM((2, HEADS, LANES), F32)])
    kern = functools.partial(_decode_kernel, layer=layer, n_chunks=n_chunks, cp=cp, page=page)
    return pl.pallas_call(
        kern, grid_spec=grid_spec,
        out_shape=[jax.ShapeDtypeStruct((b, HEADS, LANES), F32), jax.ShapeDtypeStruct((b, HEADS, LANES), F32)],
        compiler_params=_params(("arbitrary",)), name="decode",
    )(page_table, qm, qd, kn_m, kn_d, vn_d, c_lat, c_krt, c_dkt, c_dvt)


def _sample_epilogue_kernel(om_ref, od_ref, wuv_ref, lam_ref, subln_ref, omla_ref, odiff_ref, *, lam_init):
    om = [om_ref[h] for h in range(HEADS)]
    od = [od_ref[h] for h in range(HEADS)]
    omla_ref[...] = _mla_epilogue(om, wuv_ref).astype(BF16)
    chunks = _diff_epilogue(od, _diff_lambda(lam_ref, lam_init), subln_ref[...], lam_init)
    odiff_ref[:, 0:LANES] = chunks[0].astype(BF16)
    odiff_ref[:, LANES:2 * LANES] = chunks[1].astype(BF16)


def _sample_epilogue(om, od, w, lam_init):
    t = om.shape[1]
    return pl.pallas_call(
        functools.partial(_sample_epilogue_kernel, lam_init=lam_init),
        out_shape=[jax.ShapeDtypeStruct((t, 512), BF16), jax.ShapeDtypeStruct((t, 2 * LANES), BF16)],
        name="sample_epilogue",
    )(om, od, w["wuv_pad"], w["lam"], w["subln"])


def _outproj_kernel(x_ref, a_ref, b_ref, c_ref, w_ref, g_ref, xo_ref, h_ref):
    na, nb = a_ref.shape[1], b_ref.shape[1]
    y = (jnp.dot(a_ref[...], w_ref[0:na, :], preferred_element_type=F32)
         + jnp.dot(b_ref[...], w_ref[na:na + nb, :], preferred_element_type=F32)
         + jnp.dot(c_ref[...], w_ref[na + nb:, :], preferred_element_type=F32))
    xo = x_ref[...] + y
    xo_ref[...] = xo
    h_ref[...] = _rms(xo, g_ref[...], NORM_EPS).astype(BF16)


def _outproj(x, a, b, c, w_out, g, *, tm):
    t, d = x.shape
    tm = min(tm, t)
    tok = lambda n: pl.BlockSpec((tm, n), lambda i: (i, 0))
    return pl.pallas_call(
        _outproj_kernel, grid=(t // tm,),
        in_specs=[tok(d), tok(a.shape[1]), tok(b.shape[1]), tok(c.shape[1]), _full(w_out.shape), _full((1, d))],
        out_specs=[tok(d), tok(d)],
        out_shape=[jax.ShapeDtypeStruct((t, d), F32), jax.ShapeDtypeStruct((t, d), BF16)],
        compiler_params=_params(("parallel",)), name="outproj",
    )(x, a, b, c, w_out, g)


def _swiglu_tile(h, wg, wu, wd):
    g = jnp.dot(h, wg, preferred_element_type=F32)
    u = jnp.dot(h, wu, preferred_element_type=F32)
    act = (g * jax.nn.sigmoid(g) * u).astype(BF16)
    return jnp.dot(act, wd, preferred_element_type=F32)


def _ffn_kernel(x_ref, h_ref, wg_ref, wu_ref, wd_ref, gf_ref, o_ref, acc_ref, *, final_norm):
    f = pl.program_id(1)

    @pl.when(f == 0)
    def _():
        acc_ref[...] = jnp.zeros(acc_ref.shape, F32)

    acc_ref[...] += _swiglu_tile(h_ref[...], wg_ref[...], wu_ref[...], wd_ref[...])

    @pl.when(f == pl.num_programs(1) - 1)
    def _():
        y = x_ref[...] + acc_ref[...]
        o_ref[...] = _rms(y, gf_ref[...], NORM_EPS) if final_norm else y


def _ffn(x, h, wg, wu, wd, g_final, *, final_norm, tm, tf):
    t, d = x.shape
    dff = wg.shape[1]
    tm = min(tm, t)
    assert dff % tf == 0
    return pl.pallas_call(
        functools.partial(_ffn_kernel, final_norm=final_norm), grid=(t // tm, dff // tf),
        in_specs=[pl.BlockSpec((tm, d), lambda i, f: (i, 0)), pl.BlockSpec((tm, d), lambda i, f: (i, 0)),
                  pl.BlockSpec((d, tf), lambda i, f: (0, f)), pl.BlockSpec((d, tf), lambda i, f: (0, f)),
                  pl.BlockSpec((tf, d), lambda i, f: (f, 0)), pl.BlockSpec((1, d), lambda i, f: (0, 0))],
        out_specs=pl.BlockSpec((tm, d), lambda i, f: (i, 0)),
        out_shape=jax.ShapeDtypeStruct((t, d), F32),
        scratch_shapes=[pltpu.VMEM((tm, d), F32)],
        compiler_params=_params(("parallel", "arbitrary")), name="ffn",
    )(x, h, wg, wu, wd, g_final)


def _router_kernel(x_ref, g_ref, wr_ref, gate_ref, sel_ref, *, n_experts):
    h = _rms(x_ref[...], g_ref[...], NORM_EPS)
    logits = jnp.dot(h, wr_ref[...], precision=lax.Precision.HIGHEST, preferred_element_type=F32)
    lane = lax.broadcasted_iota(jnp.int32, logits.shape, 1).astype(F32)
    logits = jnp.where(lane < n_experts, logits, NEG_INF)
    gate = jnp.zeros(logits.shape, F32)
    tops, picks = [], []
    for _ in range(TOP_K):
        top = jnp.max(logits, axis=-1, keepdims=True)
        pick = jnp.min(jnp.where(logits == top, lane, float(LANES)), axis=-1, keepdims=True)
        tops.append(top)
        picks.append(pick)
        logits = jnp.where(lane == pick, NEG_INF, logits)
    e = [jnp.exp(tv - tops[0]) for tv in tops]
    den = e[0]
    for ev in e[1:]:
        den = den + ev
    sel = jnp.zeros(logits.shape, F32)
    for k, (ev, pick) in enumerate(zip(e, picks)):
        gk = ev / den
        gate = gate + jnp.where(lane == pick, gk, 0.0)
        sel = sel + jnp.where(lane == k, pick, 0.0) + jnp.where(lane == TOP_K + k, gk, 0.0)
    gate_ref[...] = gate
    sel_ref[...] = sel


def _router(x, g, wr_pad, n_experts, *, tm):
    t, d = x.shape
    tm = min(tm, t)
    return pl.pallas_call(
        functools.partial(_router_kernel, n_experts=n_experts), grid=(t // tm,),
        in_specs=[pl.BlockSpec((tm, d), lambda i: (i, 0)), _full((1, d)), _full(wr_pad.shape)],
        out_specs=[pl.BlockSpec((tm, LANES), lambda i: (i, 0))] * 2,
        out_shape=[jax.ShapeDtypeStruct((t, LANES), F32)] * 2,
        compiler_params=_params(("parallel",)), name="router",
    )(x, g, wr_pad)


def _moe_kernel(x_ref, h_ref, gate_ref, wg_ref, wu_ref, wd_ref, gf_ref, o_ref, acc_ref, *, final_norm):
    e = pl.program_id(1)
    f = pl.program_id(2)

    @pl.when((e == 0) & (f == 0))
    def _():
        acc_ref[...] = jnp.zeros(acc_ref.shape, F32)

    lane = lax.broadcasted_iota(jnp.int32, (1, LANES), 1)
    ge = jnp.sum(jnp.where(lane == e, gate_ref[...], 0.0), axis=-1, keepdims=True)
    acc_ref[...] += ge * _swiglu_tile(h_ref[...], wg_ref[...], wu_ref[...], wd_ref[...])

    @pl.when((e == pl.num_programs(1) - 1) & (f == pl.num_programs(2) - 1))
    def _():
        y = x_ref[...] + acc_ref[...]
        o_ref[...] = _rms(y, gf_ref[...], NORM_EPS) if final_norm else y


def _moe(x, h, gate, wg, wu, wd, g_final, *, final_norm, tm, tf):
    t, d = x.shape
    n_e, _, dff = wg.shape
    tm = min(tm, t)
    assert dff % tf == 0
    return pl.pallas_call(
        functools.partial(_moe_kernel, final_norm=final_norm), grid=(t // tm, n_e, dff // tf),
        in_specs=[pl.BlockSpec((tm, d), lambda i, e, f: (i, 0)), pl.BlockSpec((tm, d), lambda i, e, f: (i, 0)),
                  pl.BlockSpec((tm, LANES), lambda i, e, f: (i, 0)),
                  pl.BlockSpec((None, d, tf), lambda i, e, f: (e, 0, f)),
                  pl.BlockSpec((None, d, tf), lambda i, e, f: (e, 0, f)),
                  pl.BlockSpec((None, tf, d), lambda i, e, f: (e, f, 0)),
                  pl.BlockSpec((1, d), lambda i, e, f: (0, 0))],
        out_specs=pl.BlockSpec((tm, d), lambda i, e, f: (i, 0)),
        out_shape=jax.ShapeDtypeStruct((t, d), F32),
        scratch_shapes=[pltpu.VMEM((tm, d), F32)],
        compiler_params=_params(("parallel", "arbitrary", "arbitrary")), name="moe",
    )(x, h, gate, wg, wu, wd, g_final)


ROWS_PER_DMA_ITER = 8


def _row_gather(idx_ref, base, src_hbm, dst_buf, sem, n_rows, op):
    def row_copy(r):
        return pltpu.make_async_copy(src_hbm.at[pl.ds(idx_ref[base + r], 1), :], dst_buf.at[pl.ds(r, 1), :], sem)

    if op == "start":
        for r in range(n_rows):
            row_copy(r).start()
    else:
        def body(it, carry):
            for u in range(ROWS_PER_DMA_ITER):
                row_copy(it * ROWS_PER_DMA_ITER + u).wait()
            return carry

        lax.fori_loop(0, n_rows // ROWS_PER_DMA_ITER, body, 0)


def _experts_kernel(src_ref, te_ref, nused_ref, x_hbm, g_ref, gate_ref, wg_ref, wu_ref, wd_ref, ys_ref,
                    xbuf, sem, hbuf, acc_ref, *, tm):
    i = pl.program_id(0)
    f = pl.program_id(1)
    n_used = nused_ref[0]
    slot = i % 2
    used = i < n_used

    @pl.when((f == 0) & used)
    def _():
        @pl.when(i == 0)
        def _():
            _row_gather(src_ref, 0, x_hbm, xbuf.at[0], sem.at[0], tm, "start")

        @pl.when(i + 1 < n_used)
        def _():
            _row_gather(src_ref, (i + 1) * tm, x_hbm, xbuf.at[1 - slot], sem.at[1 - slot], tm, "start")

        _row_gather(src_ref, i * tm, x_hbm, xbuf.at[slot], sem.at[slot], tm, "wait")
        hbuf[...] = _rms(xbuf[slot], g_ref[...], NORM_EPS).astype(BF16)

    @pl.when(used)
    def _():
        y = _swiglu_tile(hbuf[...], wg_ref[...], wu_ref[...], wd_ref[...])

        @pl.when(f == 0)
        def _():
            acc_ref[...] = y

        @pl.when(f > 0)
        def _():
            acc_ref[...] += y

    @pl.when(f == pl.num_programs(1) - 1)
    def _():
        @pl.when(used)
        def _():
            ys_ref[...] = acc_ref[...] * gate_ref[...]

        @pl.when(jnp.logical_not(used))
        def _():
            ys_ref[...] = jnp.zeros(ys_ref.shape, F32)


def _combine_kernel(pos_ref, x_ref, ys_hbm, gf_ref, o_ref, ybuf, sem, *, tm, t_total, final_norm):
    i = pl.program_id(0)
    slot = i % 2

    def gather(tile, slot, op):
        for k in range(TOP_K):
            _row_gather(pos_ref, k * t_total + tile * tm, ys_hbm, ybuf.at[slot, k], sem.at[slot], tm, op)

    @pl.when(i == 0)
    def _():
        gather(0, 0, "start")

    @pl.when(i + 1 < pl.num_programs(0))
    def _():
        gather(i + 1, 1 - slot, "start")

    gather(i, slot, "wait")
    y = x_ref[...]
    for k in range(TOP_K):
        y = y + ybuf[slot, k]
    o_ref[...] = _rms(y, gf_ref[...], NORM_EPS) if final_norm else y


def _moe_routed(x, sel, g_ffn, wg, wu, wd, g_final, *, final_norm, tm_e, tf, tm_c):
    t, d = x.shape
    n_e, _, dff = wg.shape
    assert dff % tf == 0 and tm_e % ROWS_PER_DMA_ITER == 0 and t % tm_c == 0
    e_flat = sel[:, 0:TOP_K].astype(jnp.int32).T.reshape(-1)
    g_flat = sel[:, TOP_K:2 * TOP_K].T.reshape(-1)
    tok = jnp.tile(jnp.arange(t, dtype=jnp.int32), TOP_K)
    onehot = (e_flat[:, None] == jnp.arange(n_e, dtype=jnp.int32)[None]).astype(jnp.int32)
    cnt = jnp.sum(onehot, axis=0)
    rank = jnp.take_along_axis(jnp.cumsum(onehot, axis=0) - onehot, e_flat[:, None], axis=1)[:, 0]
    pad_cnt = (cnt + tm_e - 1) // tm_e * tm_e
    pend = jnp.cumsum(pad_cnt)
    pos = (jnp.take(pend - pad_cnt, e_flat) + rank).astype(jnp.int32)
    n_tiles = (TOP_K * t + n_e * (tm_e - 1)) // tm_e
    n_pad = n_tiles * tm_e
    row_src = jnp.zeros((n_pad,), jnp.int32).at[pos].set(tok)
    row_gate = jnp.zeros((n_pad,), F32).at[pos].set(g_flat)[:, None]
    tile_e = jnp.minimum(jnp.searchsorted(pend, jnp.arange(n_tiles, dtype=jnp.int32) * tm_e, side="right"),
                         n_e - 1).astype(jnp.int32)
    n_used = (pend[-1] // tm_e).astype(jnp.int32)[None]

    hbm = pl.BlockSpec(memory_space=pl.ANY)
    ys = pl.pallas_call(
        functools.partial(_experts_kernel, tm=tm_e),
        grid_spec=pltpu.PrefetchScalarGridSpec(
            num_scalar_prefetch=3, grid=(n_tiles, dff // tf),
            in_specs=[hbm, pl.BlockSpec((1, d), lambda i, f, *_: (0, 0)),
                      pl.BlockSpec((tm_e, 1), lambda i, f, *_: (i, 0)),
                      pl.BlockSpec((None, d, tf), lambda i, f, src, te, nu: (te[i], 0, f)),
                      pl.BlockSpec((None, d, tf), lambda i, f, src, te, nu: (te[i], 0, f)),
                      pl.BlockSpec((None, tf, d), lambda i, f, src, te, nu: (te[i], f, 0))],
            out_specs=pl.BlockSpec((tm_e, d), lambda i, f, *_: (i, 0)),
            scratch_shapes=[pltpu.VMEM((2, tm_e, d), F32), pltpu.SemaphoreType.DMA((2,)),
                            pltpu.VMEM((tm_e, d), BF16), pltpu.VMEM((tm_e, d), F32)]),
        out_shape=jax.ShapeDtypeStruct((n_pad, d), F32),
        compiler_params=_params(("arbitrary", "arbitrary")), name="experts",
    )(row_src, tile_e, n_used, x, g_ffn, row_gate, wg, wu, wd)

    return pl.pallas_call(
        functools.partial(_combine_kernel, tm=tm_c, t_total=t, final_norm=final_norm),
        grid_spec=pltpu.PrefetchScalarGridSpec(
            num_scalar_prefetch=1, grid=(t // tm_c,),
            in_specs=[pl.BlockSpec((tm_c, d), lambda i, *_: (i, 0)), hbm, pl.BlockSpec((1, d), lambda i, *_: (0, 0))],
            out_specs=pl.BlockSpec((tm_c, d), lambda i, *_: (i, 0)),
            scratch_shapes=[pltpu.VMEM((2, TOP_K, tm_c, d), F32), pltpu.SemaphoreType.DMA((2,))]),
        out_shape=jax.ShapeDtypeStruct((t, d), F32),
        compiler_params=_params(("arbitrary",)), name="moe_combine",
    )(pos, x, ys, g_final)


def _rot_cols(w):
    return jnp.concatenate([-w[..., ROPE_HALF:], w[..., :ROPE_HALF]], axis=-1)


def _pad_cols(w, n):
    return jnp.pad(w, ((0, 0), (0, n - w.shape[1])))


def _layer_weights(l, P):
    d = P["w_in"].shape[1]
    cq_n = P["mla_q_norm"].shape[1]
    ckv_n = P["mla_kv_norm"].shape[1]
    heads, nope = P["mla_w_uk"].shape[2], P["mla_w_uk"].shape[3]
    rope_n = P["mla_w_uq"].shape[3] - nope
    v_n = P["mla_w_uv"].shape[3]
    dh = P["diff_lambda_q1"].shape[1]
    n_groups, chunk = P["chunk_w_s"].shape[1], P["chunk_w_s"].shape[2]
    cw = P["chunk_ln_g"].shape[1]
    assert (cq_n, ckv_n, heads, rope_n, dh, n_groups, chunk, cw) == (256, 128, HEADS, 32, 32, 4, CHUNK, 256)
    dq_n, dk_n, dv_n = 256, 128, 128
    assert P["w_in"].shape[2] == cq_n + ckv_n + rope_n + dq_n + dk_n + dv_n + 2 * cw
    w_in = P["w_in"][l]
    o = 0
    parts = {}
    for name, n in (("cq", cq_n), ("ckv", ckv_n), ("kr", rope_n), ("dq", dq_n), ("dk", dk_n), ("dv", dv_n),
                    ("cu", cw), ("cv", cw)):
        parts[name] = w_in[:, o:o + n]
        o += n
    rot32 = lambda w: _rot_cols(w.reshape(d, -1, 2 * ROPE_HALF)).reshape(d, -1)
    win = jnp.concatenate([
        parts["cq"], parts["ckv"], _pad_cols(parts["kr"], LANES), _pad_cols(rot32(parts["kr"]), LANES),
        parts["dq"], rot32(parts["dq"]), parts["dk"], rot32(parts["dk"]), parts["dv"], parts["cu"], parts["cv"],
    ], axis=1).astype(BF16)

    uq = jnp.transpose(P["mla_w_uq"][l], (1, 0, 2))
    uk = jnp.transpose(P["mla_w_uk"][l], (1, 0, 2))
    wq_lat = _fold_q(uq[:, :, :nope], uk)
    uq_r = uq[:, :, nope:]
    pad_r = lambda w: jnp.transpose(jnp.pad(w, ((0, 0), (0, 0), (0, LANES - rope_n))), (1, 0, 2)).reshape(cq_n, -1)
    wq = jnp.concatenate([wq_lat, pad_r(uq_r), pad_r(_rot_cols(uq_r))], axis=1).astype(BF16)

    uv = jnp.transpose(P["mla_w_uv"][l], (1, 0, 2))
    wuv_pad = jnp.stack([jnp.pad(uv[h], ((0, 0), (h * v_n, (heads - 1 - h) * v_n))) for h in range(heads)])

    ws = P["chunk_w_s"][l]
    bs = P["chunk_b_s"][l]
    rep = lambda a: jnp.repeat(a, cw // n_groups, axis=-1)
    return dict(
        win=win, wq=wq, wuv_pad=wuv_pad.astype(BF16),
        gmix=P["norm_mix"][l][None], gq=P["mla_q_norm"][l][None], gkv=P["mla_kv_norm"][l][None],
        lng=P["chunk_ln_g"][l][None], lnb=P["chunk_ln_b"][l][None],
        ws=ws, bs_full=rep(jnp.transpose(bs)),
        ws_row0=rep(ws[:, 0, 0][None]), bs_row0=rep(bs[:, 0][None]),
        lam=jnp.stack([P["diff_lambda_q1"][l], P["diff_lambda_k1"][l], P["diff_lambda_q2"][l],
                       P["diff_lambda_k2"][l]]),
        subln=jnp.tile(P["diff_subln"][l], 2)[None],
        w_out=P["w_out"][l].astype(BF16), gffn=P["norm_ffn"][l][None],
        mla_scale=float((nope + rope_n) ** -0.5), diff_scale=float(dh ** -0.5),
    )


def _rope_tables(pos):
    inv = 1.0 / (ROPE_THETA ** (jnp.arange(ROPE_HALF, dtype=F32) / ROPE_HALF))
    ang = pos.astype(F32)[:, None] * inv[None, :]
    reps = LANES // ROPE_HALF
    return jnp.tile(jnp.cos(ang), (1, reps)), jnp.tile(jnp.sin(ang), (1, reps))


def kernel(x_prompt, x_sample, cache_mla_latent, cache_mla_krope, cache_diff_k, cache_diff_v, page_table, norm_mix, w_in, mla_q_norm, mla_w_uq, mla_kv_norm, mla_w_uk, mla_w_uv, diff_lambda_q1, diff_lambda_k1, diff_lambda_q2, diff_lambda_k2, diff_subln, chunk_ln_g, chunk_ln_b, chunk_w_s, chunk_b_s, w_out, norm_ffn, ffn_w_gate, ffn_w_up, ffn_w_down, moe_w_router, moe_w_gate, moe_w_up, moe_w_down, norm_final):
    P = dict(norm_mix=norm_mix, w_in=w_in, mla_q_norm=mla_q_norm, mla_w_uq=mla_w_uq, mla_kv_norm=mla_kv_norm,
             mla_w_uk=mla_w_uk, mla_w_uv=mla_w_uv, diff_lambda_q1=diff_lambda_q1, diff_lambda_k1=diff_lambda_k1,
             diff_lambda_q2=diff_lambda_q2, diff_lambda_k2=diff_lambda_k2, diff_subln=diff_subln,
             chunk_ln_g=chunk_ln_g, chunk_ln_b=chunk_ln_b, chunk_w_s=chunk_w_s, chunk_b_s=chunk_b_s,
             w_out=w_out, norm_ffn=norm_ffn)
    depth = w_in.shape[0]
    bp, s, d = x_prompt.shape
    bs, ts, _ = x_sample.shape
    assert bp == 1 and ts == 1
    n_pool, page = cache_mla_latent.shape[1], cache_mla_latent.shape[2]
    past_len = page_table.shape[1] * page
    assert past_len % CHUNK == 0
    n_experts = moe_w_router.shape[2]

    cos_p, sin_p = _rope_tables(jnp.arange(s))
    cos_s, sin_s = _rope_tables(jnp.full((bs,), past_len))
    c_krt = jnp.transpose(cache_mla_krope, (0, 1, 3, 2))
    c_dkt = jnp.transpose(cache_diff_k, (0, 1, 3, 4, 5, 2)).reshape(depth, n_pool, LANES, page)
    c_dvt = jnp.transpose(cache_diff_v, (0, 1, 3, 4, 2)).reshape(depth, n_pool, LANES, page)
    gfin = norm_final[None]

    xp = x_prompt.reshape(s, d)
    xs = x_sample.reshape(bs, d)
    st_p, st_s = [], []
    for l in range(depth):
        w = _layer_weights(l, P)
        lam_init = 0.8 - 0.6 * math.exp(-0.3 * l)
        last = l == depth - 1
        j = l // 2
        if l % 2 == 0:
            mix_w = (ffn_w_gate[j].astype(BF16), ffn_w_up[j].astype(BF16), ffn_w_down[j].astype(BF16))
        else:
            mix_w = (moe_w_gate[j].astype(BF16), moe_w_up[j].astype(BF16), moe_w_down[j].astype(BF16))
            wr_pad = _pad_cols(moe_w_router[j], LANES)

        def channel_mix(x, h, tm):
            if l % 2 == 0:
                return _ffn(x, h, *mix_w, gfin, final_norm=last, tm=tm, tf=mix_w[0].shape[1] // 2)
            gate, sel = _router(x, w["gffn"], wr_pad, n_experts, tm=tm)
            tf = mix_w[0].shape[2] // 2
            tm_e = 512
            if x.shape[0] * TOP_K < n_experts * tm_e:
                return _moe(x, h, gate, *mix_w, gfin, final_norm=last, tm=tm, tf=tf)
            return _moe_routed(x, sel, w["gffn"], *mix_w, gfin, final_norm=last, tm_e=tm_e, tf=tf, tm_c=256)

        qm, kc, qd, kd, vd, oc, ckv, kr, dk, dv, vst = _prep(xs, cos_s, sin_s, w, sample_mode=True, tm=128)
        om, od = _decode(l, page_table, jnp.transpose(qm, (1, 0, 2)), jnp.transpose(qd, (1, 0, 2)),
                         kc[:, None], kd[:, None], vd[:, None],
                         cache_mla_latent, c_krt, c_dkt, c_dvt, cp=32)
        o_mla, o_diff = _sample_epilogue(jnp.transpose(om, (1, 0, 2)), jnp.transpose(od, (1, 0, 2)), w, lam_init)
        xs, hs = _outproj(xs, o_mla, o_diff, oc, w["w_out"], w["gffn"], tm=128)
        xs = channel_mix(xs, hs, 128)
        st_s.append((ckv, kr, dk, dv, vst))

        qm, kc, qd, kd, _, oc, ckv, kr, dk, dv, vme, vde = _prep(xp, cos_p, sin_p, w, sample_mode=False, tm=256)
        o_mla = _flash(qm, kc, vme, (w["wuv_pad"],), mode="mla", lam_init=lam_init, tq=512, tk=1024, groups=8)
        o_diff = _flash(qd, kd, vde, (w["lam"], w["subln"]), mode="diff", lam_init=lam_init, tq=512, tk=2048,
                        groups=8)
        xp, hp = _outproj(xp, o_mla, o_diff, oc, w["w_out"], w["gffn"], tm=512)
        xp = channel_mix(xp, hp, 512)
        st_p.append((ckv, kr, dk, dv))

    stack = lambda sts, i, shape: jnp.stack([st[i] for st in sts], axis=0).reshape(shape)
    return (xp.reshape(bp, s, d), xs.reshape(bs, ts, d),
            stack(st_p, 0, (depth, bp, s, -1)), stack(st_p, 1, (depth, bp, s, -1)),
            stack(st_p, 2, (depth, bp, s) + cache_diff_k.shape[3:]),
            stack(st_p, 3, (depth, bp, s) + cache_diff_v.shape[3:]),
            stack(st_s, 0, (depth, bs, ts, -1)), stack(st_s, 1, (depth, bs, ts, -1)),
            stack(st_s, 2, (depth, bs, ts) + cache_diff_k.shape[3:]),
            stack(st_s, 3, (depth, bs, ts) + cache_diff_v.shape[3:]),
            stack(st_s, 4, (depth, bs, ts, -1)))
```

```python
import functools
import math

import jax
import jax.numpy as jnp
from jax import lax
from jax.experimental import pallas as pl
from jax.experimental.pallas import tpu as pltpu

F32 = jnp.float32
BF16 = jnp.bfloat16

ROPE_THETA = 10000.0
NORM_EPS = 1e-6
LN_EPS = 1e-5
NEG_INF = -1e30
TOP_K = 2
LOG2E = math.log2(math.e)

LANES = 128
HEADS = 8
ROPE_HALF = 16
CHUNK = 128
VMEM_LIMIT = 56 * 1024 * 1024
DECODE_SLOTS = 3


def _params(sem):
    return pltpu.CompilerParams(dimension_semantics=sem, vmem_limit_bytes=VMEM_LIMIT)


def _rms(x, g, eps):
    return x * lax.rsqrt(jnp.mean(x * x, axis=-1, keepdims=True) + eps) * g


def _gelu(x):
    return 0.5 * x * (1.0 + jnp.tanh(math.sqrt(2.0 / math.pi) * (x + 0.044715 * (x * x * x))))


def _full(shape):
    return pl.BlockSpec(shape, lambda *_: (0,) * len(shape))


def _fold_q_kernel(uq_ref, uk_ref, o_ref):
    for h in range(HEADS):
        o_ref[:, h * LANES:(h + 1) * LANES] = lax.dot_general(
            uq_ref[h], uk_ref[h], (((1,), (1,)), ((), ())),
            precision=lax.Precision.HIGHEST, preferred_element_type=F32)


def _fold_q(uq_nope, uk):
    h, cq, _ = uq_nope.shape
    return pl.pallas_call(
        _fold_q_kernel,
        out_shape=jax.ShapeDtypeStruct((cq, h * uk.shape[1]), F32),
        name="fold_q",
    )(uq_nope, uk)


def _prep_kernel(x_ref, gmix_ref, win_ref, gq_ref, wq_ref, gkv_ref, cos_ref, sin_ref, lng_ref, lnb_ref,
                 ws_ref, bs_ref,
                 qm_ref, kc_ref, qd_ref, kd_ref, vd_ref, oc_ref, ckv_ref, kr_ref, dk_ref, dv_ref, *rest,
                 sample_mode, mla_scale, diff_scale):
    tm = x_ref.shape[0]
    x = x_ref[...]
    xn = _rms(x, gmix_ref[...], NORM_EPS).astype(BF16)
    z = jnp.dot(xn, win_ref[...], preferred_element_type=F32)
    cos = cos_ref[...]
    sin = sin_ref[...]
    col = lambda a, n=1: z[:, a * LANES:(a + n) * LANES]

    cqn = _rms(col(0, 2), gq_ref[...], NORM_EPS).astype(BF16)
    qall = jnp.dot(cqn, wq_ref[...], preferred_element_type=F32)
    for h in range(HEADS):
        qlat = qall[:, h * LANES:(h + 1) * LANES]
        qa = qall[:, (HEADS + h) * LANES:(HEADS + h + 1) * LANES]
        qb = qall[:, (2 * HEADS + h) * LANES:(2 * HEADS + h + 1) * LANES]
        qm_ref[h, :, 0:LANES] = (qlat * mla_scale).astype(BF16)
        qm_ref[h, :, LANES:2 * LANES] = ((qa * cos + qb * sin) * mla_scale).astype(BF16)
    ckvn = _rms(col(2), gkv_ref[...], NORM_EPS)
    kr = col(3) * cos + col(4) * sin
    ckv_ref[...] = ckvn
    kr_ref[...] = kr[:, 0:2 * ROPE_HALF]
    kc_ref[:, 0:LANES] = ckvn.astype(BF16)
    kc_ref[:, LANES:2 * LANES] = kr.astype(BF16)

    lane = lax.broadcasted_iota(jnp.int32, (1, LANES), 1)
    for g in range(2):
        chunk = (col(5 + g) * cos + col(7 + g) * sin) * diff_scale
        rolled = pltpu.roll(chunk, LANES // 2, axis=1)
        for r in range(2):
            src = chunk if r == g else rolled
            for m in range(2):
                lo = g * 64 + m * 32
                sel = (lane >= lo) & (lane < lo + 32)
                qd_ref[g * 4 + r * 2 + m] = jnp.where(sel, src, 0.0).astype(BF16)
    dk = col(9) * cos + col(10) * sin
    dv = col(11)
    dk_ref[...] = dk
    dv_ref[...] = dv
    kd_ref[...] = dk.astype(BF16)
    vd_ref[...] = dv.astype(BF16)

    u = _gelu(col(12, 2))
    gv = _gelu(col(14, 2))
    mu = jnp.mean(gv, axis=-1, keepdims=True)
    gc = gv - mu
    var = jnp.mean(gc * gc, axis=-1, keepdims=True)
    v = gc * lax.rsqrt(var + LN_EPS) * lng_ref[...] + lnb_ref[...]
    if sample_mode:
        rest[0][...] = v
        oc_ref[...] = (u * (v * ws_ref[...] + bs_ref[...])).astype(BF16)
    else:
        vme_ref, vde_ref = rest
        ones_col = jnp.broadcast_to(jnp.where(lane == 0, 1.0, 0.0), (tm, LANES)).astype(BF16)
        vme_ref[:, 0:LANES] = ckvn.astype(BF16)
        vme_ref[:, LANES:2 * LANES] = ones_col
        vde_ref[:, 0:LANES] = dv.astype(BF16)
        vde_ref[:, LANES:2 * LANES] = ones_col
        lane2 = lax.broadcasted_iota(jnp.int32, (1, 2 * LANES), 1)
        row = lax.broadcasted_iota(jnp.int32, (CHUNK, CHUNK), 0)
        cl = lax.broadcasted_iota(jnp.int32, (CHUNK, CHUNK), 1)
        wts = [jnp.where(row >= cl, ws_ref[g], 0.0).astype(BF16) for g in range(4)]
        for c in range(tm // CHUNK):
            vc = v[c * CHUNK:(c + 1) * CHUNK]
            mix = bs_ref[...]
            for g in range(4):
                vg = jnp.where((lane2 >= g * 64) & (lane2 < (g + 1) * 64), vc, 0.0).astype(BF16)
                mix = mix + jnp.dot(wts[g], vg, preferred_element_type=F32)
            oc_ref[c * CHUNK:(c + 1) * CHUNK, :] = (u[c * CHUNK:(c + 1) * CHUNK] * mix).astype(BF16)


def _prep(x, cos, sin, w, *, sample_mode, tm):
    t, d = x.shape
    tm = min(tm, t)
    nin = w["win"].shape[1]
    nq = w["wq"].shape[1]
    tok = lambda n: pl.BlockSpec((tm, n), lambda i: (i, 0))
    heads = lambda n: pl.BlockSpec((HEADS, tm, n), lambda i: (0, i, 0))
    ws, bs = (w["ws_row0"], w["bs_row0"]) if sample_mode else (w["ws"], w["bs_full"])
    in_specs = [tok(d), _full((1, d)), _full((d, nin)), _full((1, 2 * LANES)), _full((2 * LANES, nq)),
                _full((1, LANES)), tok(LANES), tok(LANES), _full((1, 2 * LANES)), _full((1, 2 * LANES)),
                _full(ws.shape), _full(bs.shape)]
    out_shape = [jax.ShapeDtypeStruct((HEADS, t, 2 * LANES), BF16),
                 jax.ShapeDtypeStruct((t, 2 * LANES), BF16),
                 jax.ShapeDtypeStruct((HEADS, t, LANES), BF16),
                 jax.ShapeDtypeStruct((t, LANES), BF16),
                 jax.ShapeDtypeStruct((t, LANES), BF16),
                 jax.ShapeDtypeStruct((t, 2 * LANES), BF16),
                 jax.ShapeDtypeStruct((t, LANES), F32),
                 jax.ShapeDtypeStruct((t, 2 * ROPE_HALF), F32),
                 jax.ShapeDtypeStruct((t, LANES), F32),
                 jax.ShapeDtypeStruct((t, LANES), F32)]
    out_specs = [heads(2 * LANES), tok(2 * LANES), heads(LANES), tok(LANES), tok(LANES), tok(2 * LANES),
                 tok(LANES), tok(2 * ROPE_HALF), tok(LANES), tok(LANES)]
    if sample_mode:
        out_shape.append(jax.ShapeDtypeStruct((t, 2 * LANES), F32))
        out_specs.append(tok(2 * LANES))
    else:
        out_shape += [jax.ShapeDtypeStruct((t, 2 * LANES), BF16)] * 2
        out_specs += [tok(2 * LANES)] * 2
    kern = functools.partial(_prep_kernel, sample_mode=sample_mode, mla_scale=w["mla_scale"] * LOG2E,
                             diff_scale=w["diff_scale"] * LOG2E)
    return pl.pallas_call(
        kern, grid=(t // tm,), in_specs=in_specs, out_specs=out_specs, out_shape=out_shape,
        compiler_params=_params(("parallel",)), name="prep_sample" if sample_mode else "prep_prompt",
    )(x, w["gmix"], w["win"], w["gq"], w["wq"], w["gkv"], cos, sin, w["lng"], w["lnb"], ws, bs)


def _mla_epilogue(o, wuv_ref):
    out = None
    for h in range(HEADS):
        part = jnp.dot(o[h].astype(BF16), wuv_ref[h], preferred_element_type=F32)
        out = part if out is None else out + part
    return out


def _diff_lambda(lam_ref, lam_init):
    p = lam_ref[...]
    s1 = jnp.sum(p[0:1] * p[1:2], axis=-1, keepdims=True)
    s2 = jnp.sum(p[2:3] * p[3:4], axis=-1, keepdims=True)
    return jnp.exp(s1) - jnp.exp(s2) + lam_init


def _diff_epilogue(o, lam, subln, lam_init):
    lane = lax.broadcasted_iota(jnp.int32, (1, LANES), 1)
    chunks = []
    for g in range(2):
        valid = (lane >= g * 64) & (lane < (g + 1) * 64)
        halves = []
        for r in range(2):
            d = o[g * 4 + r * 2] - lam * o[g * 4 + r * 2 + 1]
            ms = jnp.sum(jnp.where(valid, d * d, 0.0), axis=-1, keepdims=True) * (1.0 / 64.0)
            y = d * lax.rsqrt(ms + LN_EPS) * subln * (1.0 - lam_init)
            halves.append(y if r == g else pltpu.roll(y, LANES // 2, axis=1))
        chunks.append(jnp.where(lane < 64, halves[0], halves[1]))
    return chunks


def _flash_kernel(q_ref, kt_ref, v_ref, *rest, tq, tk, groups, mode, lam_init):
    if mode == "mla":
        wuv_ref, out_ref, m_ref, acc_ref = rest
    else:
        lam_ref, subln_ref, out_ref, m_ref, acc_ref = rest
    i = pl.program_id(0)
    hg = HEADS // groups
    rows = hg * tq
    m_ref[...] = jnp.full(m_ref.shape, NEG_INF, F32)
    acc_ref[...] = jnp.zeros(acc_ref.shape, F32)

    def step(j, masked):
        kt = kt_ref[j]
        v = v_ref[pl.ds(pl.multiple_of(j * tk, tk), tk), :]
        if masked:
            kpos = j * tk + lax.broadcasted_iota(jnp.int32, (1, tk), 1)
            qpos = i * tq + (lax.broadcasted_iota(jnp.int32, (rows, 1), 0) & (tq - 1))
            keep = kpos <= qpos
        for g in range(groups):
            q = q_ref[g * hg:(g + 1) * hg].reshape(rows, q_ref.shape[2])
            s = jnp.dot(q, kt, preferred_element_type=F32)
            if masked:
                s = jnp.where(keep, s, NEG_INF)
            m_prev = m_ref[g]
            m_new = jnp.maximum(m_prev, jnp.max(s, axis=-1, keepdims=True))
            alpha = jnp.exp2(m_prev - m_new)
            p = jnp.exp2(s - m_new).astype(BF16)
            acc_ref[g] = alpha * acc_ref[g] + jnp.dot(p, v, preferred_element_type=F32)
            m_ref[g] = m_new

    n_full = (i * tq) // tk

    def body(j, carry):
        step(j, False)
        return carry

    lax.fori_loop(0, n_full, body, 0)
    step(n_full, True)

    o = []
    for g in range(groups):
        acc = acc_ref[g]
        og = acc[:, 0:LANES] / acc[:, LANES:LANES + 1]
        o += [og[h * tq:(h + 1) * tq] for h in range(hg)]
    if mode == "mla":
        out_ref[...] = _mla_epilogue(o, wuv_ref).astype(BF16)
    else:
        chunks = _diff_epilogue(o, _diff_lambda(lam_ref, lam_init), subln_ref[...], lam_init)
        out_ref[:, 0:LANES] = chunks[0].astype(BF16)
        out_ref[:, LANES:2 * LANES] = chunks[1].astype(BF16)


def _flash(q, k, v_ext, extra, *, mode, lam_init, tq, tk, groups):
    _, s, dk = q.shape
    tq, tk = min(tq, s), min(tk, s)
    assert tk % tq == 0 and s % tk == 0 and tq & (tq - 1) == 0 and HEADS % groups == 0
    kt = jnp.transpose(k.reshape(s // tk, tk, dk), (0, 2, 1))
    in_specs = [pl.BlockSpec((HEADS, tq, dk), lambda i: (0, i, 0)), _full(kt.shape), _full(v_ext.shape)]
    in_specs += [_full(e.shape) for e in extra]
    nout = 512 if mode == "mla" else 2 * LANES
    rows = HEADS // groups * tq
    kern = functools.partial(_flash_kernel, tq=tq, tk=tk, groups=groups, mode=mode, lam_init=lam_init)
    return pl.pallas_call(
        kern, grid=(s // tq,), in_specs=in_specs,
        out_specs=pl.BlockSpec((tq, nout), lambda i: (i, 0)),
        out_shape=jax.ShapeDtypeStruct((s, nout), BF16),
        scratch_shapes=[pltpu.VMEM((groups, rows, 1), F32), pltpu.VMEM((groups, rows, 2 * LANES), F32)],
        compiler_params=_params(("parallel",)), name="flash_" + mode,
    )(q, kt, v_ext, *extra)


def _decode_kernel(pt_ref, qm_ref, qd_ref, knm_ref, knd_ref, vnd_ref, lat_hbm, kr_hbm, dk_hbm, dv_hbm,
                   om_ref, od_ref, lat_buf, kr_buf, dk_buf, dv_buf, sem, m_ref, l_ref, acc_ref,
                   *, layer, n_chunks, n_steps, cp, page):
    t = pl.program_id(0)
    c = t % n_chunks
    n_slots = lat_buf.shape[0]
    ahead = n_slots - 1
    slot = t % n_slots

    def chunk_copies(step, slot):
        b = step // n_chunks
        first = (step % n_chunks) * cp
        out = []
        for i in range(cp):
            pg = pt_ref[b, first + i]
            tok = pl.ds(i * page, page)
            out.append(pltpu.make_async_copy(lat_hbm.at[layer, pg], lat_buf.at[slot, tok, :], sem.at[slot]))
            out.append(pltpu.make_async_copy(kr_hbm.at[layer, pg], kr_buf.at[slot, :, tok], sem.at[slot]))
            out.append(pltpu.make_async_copy(dk_hbm.at[layer, pg], dk_buf.at[slot, :, tok], sem.at[slot]))
            out.append(pltpu.make_async_copy(dv_hbm.at[layer, pg], dv_buf.at[slot, :, tok], sem.at[slot]))
        return out

    @pl.when(t == 0)
    def _():
        for first in range(min(ahead, n_steps)):
            for cpy in chunk_copies(first, first):
                cpy.start()

    @pl.when(t + ahead < n_steps)
    def _():
        for cpy in chunk_copies(t + ahead, (t + ahead) % n_slots):
            cpy.start()

    for cpy in chunk_copies(t, slot):
        cpy.wait()

    @pl.when(c == 0)
    def _():
        m_ref[...] = jnp.full(m_ref.shape, NEG_INF, F32)
        l_ref[...] = jnp.zeros(l_ref.shape, F32)
        acc_ref[...] = jnp.zeros(acc_ref.shape, F32)

    nt = (((1,), (1,)), ((), ()))
    qm = qm_ref[...]
    qd = qd_ref[...]
    lat = lat_buf[slot].astype(BF16)
    s_m = (lax.dot_general(qm[:, 0:LANES], lat, nt, preferred_element_type=F32)
           + jnp.dot(qm[:, LANES:LANES + 2 * ROPE_HALF], kr_buf[slot].astype(BF16), preferred_element_type=F32))
    s_d = jnp.dot(qd, dk_buf[slot].astype(BF16), preferred_element_type=F32)

    def update(a, s, pv):
        m_prev = m_ref[a]
        m_new = jnp.maximum(m_prev, jnp.max(s, axis=-1, keepdims=True))
        alpha = jnp.exp2(m_prev - m_new)
        pr = jnp.exp2(s - m_new)
        l_ref[a] = alpha * l_ref[a] + jnp.sum(pr, axis=-1, keepdims=True)
        acc_ref[a] = alpha * acc_ref[a] + pv(pr.astype(BF16))
        m_ref[a] = m_new

    update(0, s_m, lambda pr: jnp.dot(pr, lat, preferred_element_type=F32))
    update(1, s_d, lambda pr: lax.dot_general(pr, dv_buf[slot].astype(BF16), nt, preferred_element_type=F32))

    @pl.when(c == n_chunks - 1)
    def _():
        def finish(a, q, kn, vn, o_ref):
            s = jnp.sum(q.astype(F32) * kn.astype(F32), axis=-1, keepdims=True)
            m_prev = m_ref[a]
            m_new = jnp.maximum(m_prev, s)
            alpha = jnp.exp2(m_prev - m_new)
            pr = jnp.exp2(s - m_new)
            l = alpha * l_ref[a] + pr
            acc = alpha * acc_ref[a] + pr * vn.astype(F32)
            o_ref[...] = acc / l

        finish(0, qm, knm_ref[...], knm_ref[:, 0:LANES], om_ref)
        finish(1, qd, knd_ref[...], vnd_ref[...], od_ref)


def _decode(layer, page_table, qm, qd, kn_m, kn_d, vn_d, c_lat, c_krt, c_dkt, c_dvt, *, cp):
    b, n_pages = page_table.shape
    page = c_lat.shape[2]
    cp = min(cp, n_pages)
    assert n_pages % cp == 0
    n_chunks = n_pages // cp
    keys = cp * page
    seq = lambda n, w: pl.BlockSpec((None, n, w), lambda t, pt: (t // n_chunks, 0, 0))
    hbm = pl.BlockSpec(memory_space=pl.ANY)
    grid_spec = pltpu.PrefetchScalarGridSpec(
        num_scalar_prefetch=1, grid=(b * n_chunks,),
        in_specs=[seq(HEADS, 2 * LANES), seq(HEADS, LANES), seq(1, 2 * LANES), seq(1, LANES), seq(1, LANES),
                  hbm, hbm, hbm, hbm],
        out_specs=[seq(HEADS, LANES), seq(HEADS, LANES)],
        scratch_shapes=[pltpu.VMEM((DECODE_SLOTS, keys, LANES), F32),
                        pltpu.VMEM((DECODE_SLOTS, c_krt.shape[2], keys), F32),
                        pltpu.VMEM((DECODE_SLOTS, LANES, keys), F32), pltpu.VMEM((DECODE_SLOTS, LANES, keys), F32),
                        pltpu.SemaphoreType.DMA((DECODE_SLOTS,)),
                        pltpu.VMEM((2, HEADS, 1), F32), pltpu.VMEM((2, HEADS, 1), F32),
                        pltpu.VMEM((2, HEADS, LANES), F32)])
    kern = functools.partial(_decode_kernel, layer=layer, n_chunks=n_chunks, n_steps=b * n_chunks, cp=cp, page=page)
    return pl.pallas_call(
        kern, grid_spec=grid_spec,
        out_shape=[jax.ShapeDtypeStruct((b, HEADS, LANES), F32), jax.ShapeDtypeStruct((b, HEADS, LANES), F32)],
        compiler_params=_params(("arbitrary",)), name="decode",
    )(page_table, qm, qd, kn_m, kn_d, vn_d, c_lat, c_krt, c_dkt, c_dvt)


def _sample_epilogue_kernel(om_ref, od_ref, wuv_ref, lam_ref, subln_ref, omla_ref, odiff_ref, *, lam_init):
    om = [om_ref[h] for h in range(HEADS)]
    od = [od_ref[h] for h in range(HEADS)]
    omla_ref[...] = _mla_epilogue(om, wuv_ref).astype(BF16)
    chunks = _diff_epilogue(od, _diff_lambda(lam_ref, lam_init), subln_ref[...], lam_init)
    odiff_ref[:, 0:LANES] = chunks[0].astype(BF16)
    odiff_ref[:, LANES:2 * LANES] = chunks[1].astype(BF16)


def _sample_epilogue(om, od, w, lam_init):
    t = om.shape[1]
    return pl.pallas_call(
        functools.partial(_sample_epilogue_kernel, lam_init=lam_init),
        out_shape=[jax.ShapeDtypeStruct((t, 512), BF16), jax.ShapeDtypeStruct((t, 2 * LANES), BF16)],
        name="sample_epilogue",
    )(om, od, w["wuv_pad"], w["lam"], w["subln"])


def _outproj_kernel(x_ref, a_ref, b_ref, c_ref, w_ref, g_ref, xo_ref, h_ref):
    na, nb = a_ref.shape[1], b_ref.shape[1]
    y = (jnp.dot(a_ref[...], w_ref[0:na, :], preferred_element_type=F32)
         + jnp.dot(b_ref[...], w_ref[na:na + nb, :], preferred_element_type=F32)
         + jnp.dot(c_ref[...], w_ref[na + nb:, :], preferred_element_type=F32))
    xo = x_ref[...] + y
    xo_ref[...] = xo
    h_ref[...] = _rms(xo, g_ref[...], NORM_EPS).astype(BF16)


def _outproj(x, a, b, c, w_out, g, *, tm):
    t, d = x.shape
    tm = min(tm, t)
    tok = lambda n: pl.BlockSpec((tm, n), lambda i: (i, 0))
    return pl.pallas_call(
        _outproj_kernel, grid=(t // tm,),
        in_specs=[tok(d), tok(a.shape[1]), tok(b.shape[1]), tok(c.shape[1]), _full(w_out.shape), _full((1, d))],
        out_specs=[tok(d), tok(d)],
        out_shape=[jax.ShapeDtypeStruct((t, d), F32), jax.ShapeDtypeStruct((t, d), BF16)],
        compiler_params=_params(("parallel",)), name="outproj",
    )(x, a, b, c, w_out, g)


def _swiglu_tile(h, wg, wu, wd):
    g = jnp.dot(h, wg, preferred_element_type=F32)
    u = jnp.dot(h, wu, preferred_element_type=F32)
    act = (g * jax.nn.sigmoid(g) * u).astype(BF16)
    return jnp.dot(act, wd, preferred_element_type=F32)


def _ffn_kernel(x_ref, h_ref, wg_ref, wu_ref, wd_ref, gf_ref, o_ref, acc_ref, *, final_norm):
    f = pl.program_id(1)

    @pl.when(f == 0)
    def _():
        acc_ref[...] = jnp.zeros(acc_ref.shape, F32)

    acc_ref[...] += _swiglu_tile(h_ref[...], wg_ref[...], wu_ref[...], wd_ref[...])

    @pl.when(f == pl.num_programs(1) - 1)
    def _():
        y = x_ref[...] + acc_ref[...]
        o_ref[...] = _rms(y, gf_ref[...], NORM_EPS) if final_norm else y


def _ffn(x, h, wg, wu, wd, g_final, *, final_norm, tm, tf):
    t, d = x.shape
    dff = wg.shape[1]
    tm = min(tm, t)
    assert dff % tf == 0
    return pl.pallas_call(
        functools.partial(_ffn_kernel, final_norm=final_norm), grid=(t // tm, dff // tf),
        in_specs=[pl.BlockSpec((tm, d), lambda i, f: (i, 0)), pl.BlockSpec((tm, d), lambda i, f: (i, 0)),
                  pl.BlockSpec((d, tf), lambda i, f: (0, f)), pl.BlockSpec((d, tf), lambda i, f: (0, f)),
                  pl.BlockSpec((tf, d), lambda i, f: (f, 0)), pl.BlockSpec((1, d), lambda i, f: (0, 0))],
        out_specs=pl.BlockSpec((tm, d), lambda i, f: (i, 0)),
        out_shape=jax.ShapeDtypeStruct((t, d), F32),
        scratch_shapes=[pltpu.VMEM((tm, d), F32)],
        compiler_params=_params(("parallel", "arbitrary")), name="ffn",
    )(x, h, wg, wu, wd, g_final)


def _router_kernel(x_ref, g_ref, wr_ref, gate_ref, sel_ref, *, n_experts):
    h = _rms(x_ref[...], g_ref[...], NORM_EPS)
    logits = jnp.dot(h, wr_ref[...], precision=lax.Precision.HIGHEST, preferred_element_type=F32)
    lane = lax.broadcasted_iota(jnp.int32, logits.shape, 1).astype(F32)
    logits = jnp.where(lane < n_experts, logits, NEG_INF)
    gate = jnp.zeros(logits.shape, F32)
    tops, picks = [], []
    for _ in range(TOP_K):
        top = jnp.max(logits, axis=-1, keepdims=True)
        pick = jnp.min(jnp.where(logits == top, lane, float(LANES)), axis=-1, keepdims=True)
        tops.append(top)
        picks.append(pick)
        logits = jnp.where(lane == pick, NEG_INF, logits)
    e = [jnp.exp(tv - tops[0]) for tv in tops]
    den = e[0]
    for ev in e[1:]:
        den = den + ev
    sel = jnp.zeros(logits.shape, F32)
    for k, (ev, pick) in enumerate(zip(e, picks)):
        gk = ev / den
        gate = gate + jnp.where(lane == pick, gk, 0.0)
        sel = sel + jnp.where(lane == k, pick, 0.0) + jnp.where(lane == TOP_K + k, gk, 0.0)
    gate_ref[...] = gate
    sel_ref[...] = sel


def _router(x, g, wr_pad, n_experts, *, tm):
    t, d = x.shape
    tm = min(tm, t)
    return pl.pallas_call(
        functools.partial(_router_kernel, n_experts=n_experts), grid=(t // tm,),
        in_specs=[pl.BlockSpec((tm, d), lambda i: (i, 0)), _full((1, d)), _full(wr_pad.shape)],
        out_specs=[pl.BlockSpec((tm, LANES), lambda i: (i, 0))] * 2,
        out_shape=[jax.ShapeDtypeStruct((t, LANES), F32)] * 2,
        compiler_params=_params(("parallel",)), name="router",
    )(x, g, wr_pad)


def _moe_kernel(x_ref, h_ref, gate_ref, wg_ref, wu_ref, wd_ref, gf_ref, o_ref, acc_ref, *, final_norm):
    e = pl.program_id(1)
    f = pl.program_id(2)

    @pl.when((e == 0) & (f == 0))
    def _():
        acc_ref[...] = jnp.zeros(acc_ref.shape, F32)

    lane = lax.broadcasted_iota(jnp.int32, (1, LANES), 1)
    ge = jnp.sum(jnp.where(lane == e, gate_ref[...], 0.0), axis=-1, keepdims=True)
    acc_ref[...] += ge * _swiglu_tile(h_ref[...], wg_ref[...], wu_ref[...], wd_ref[...])

    @pl.when((e == pl.num_programs(1) - 1) & (f == pl.num_programs(2) - 1))
    def _():
        y = x_ref[...] + acc_ref[...]
        o_ref[...] = _rms(y, gf_ref[...], NORM_EPS) if final_norm else y


def _moe(x, h, gate, wg, wu, wd, g_final, *, final_norm, tm, tf):
    t, d = x.shape
    n_e, _, dff = wg.shape
    tm = min(tm, t)
    assert dff % tf == 0
    return pl.pallas_call(
        functools.partial(_moe_kernel, final_norm=final_norm), grid=(t // tm, n_e, dff // tf),
        in_specs=[pl.BlockSpec((tm, d), lambda i, e, f: (i, 0)), pl.BlockSpec((tm, d), lambda i, e, f: (i, 0)),
                  pl.BlockSpec((tm, LANES), lambda i, e, f: (i, 0)),
                  pl.BlockSpec((None, d, tf), lambda i, e, f: (e, 0, f)),
                  pl.BlockSpec((None, d, tf), lambda i, e, f: (e, 0, f)),
                  pl.BlockSpec((None, tf, d), lambda i, e, f: (e, f, 0)),
                  pl.BlockSpec((1, d), lambda i, e, f: (0, 0))],
        out_specs=pl.BlockSpec((tm, d), lambda i, e, f: (i, 0)),
        out_shape=jax.ShapeDtypeStruct((t, d), F32),
        scratch_shapes=[pltpu.VMEM((tm, d), F32)],
        compiler_params=_params(("parallel", "arbitrary", "arbitrary")), name="moe",
    )(x, h, gate, wg, wu, wd, g_final)


ROWS_PER_DMA_ITER = 8


def _row_gather(idx_ref, base, src_hbm, dst_buf, sem, n_rows, op):
    def row_copy(r):
        return pltpu.make_async_copy(src_hbm.at[pl.ds(idx_ref[base + r], 1), :], dst_buf.at[pl.ds(r, 1), :], sem)

    if op == "start":
        for r in range(n_rows):
            row_copy(r).start()
    else:
        def body(it, carry):
            for u in range(ROWS_PER_DMA_ITER):
                row_copy(it * ROWS_PER_DMA_ITER + u).wait()
            return carry

        lax.fori_loop(0, n_rows // ROWS_PER_DMA_ITER, body, 0)


def _experts_kernel(src_ref, te_ref, nused_ref, x_hbm, g_ref, gate_ref, wg_ref, wu_ref, wd_ref, ys_ref,
                    xbuf, sem, hbuf, acc_ref, *, tm):
    i = pl.program_id(0)
    f = pl.program_id(1)
    n_used = nused_ref[0]
    slot = i % 2
    used = i < n_used

    @pl.when((f == 0) & used)
    def _():
        @pl.when(i == 0)
        def _():
            _row_gather(src_ref, 0, x_hbm, xbuf.at[0], sem.at[0], tm, "start")

        @pl.when(i + 1 < n_used)
        def _():
            _row_gather(src_ref, (i + 1) * tm, x_hbm, xbuf.at[1 - slot], sem.at[1 - slot], tm, "start")

        _row_gather(src_ref, i * tm, x_hbm, xbuf.at[slot], sem.at[slot], tm, "wait")
        hbuf[...] = _rms(xbuf[slot], g_ref[...], NORM_EPS).astype(BF16)

    @pl.when(used)
    def _():
        y = _swiglu_tile(hbuf[...], wg_ref[...], wu_ref[...], wd_ref[...])

        @pl.when(f == 0)
        def _():
            acc_ref[...] = y

        @pl.when(f > 0)
        def _():
            acc_ref[...] += y

    @pl.when(f == pl.num_programs(1) - 1)
    def _():
        @pl.when(used)
        def _():
            ys_ref[...] = acc_ref[...] * gate_ref[...]

        @pl.when(jnp.logical_not(used))
        def _():
            ys_ref[...] = jnp.zeros(ys_ref.shape, F32)


def _combine_kernel(pos_ref, x_ref, ys_hbm, gf_ref, o_ref, ybuf, sem, *, tm, t_total, final_norm):
    i = pl.program_id(0)
    slot = i % 2

    def gather(tile, slot, op):
        for k in range(TOP_K):
            _row_gather(pos_ref, k * t_total + tile * tm, ys_hbm, ybuf.at[slot, k], sem.at[slot], tm, op)

    @pl.when(i == 0)
    def _():
        gather(0, 0, "start")

    @pl.when(i + 1 < pl.num_programs(0))
    def _():
        gather(i + 1, 1 - slot, "start")

    gather(i, slot, "wait")
    y = x_ref[...]
    for k in range(TOP_K):
        y = y + ybuf[slot, k]
    o_ref[...] = _rms(y, gf_ref[...], NORM_EPS) if final_norm else y


def _moe_routed(x, sel, g_ffn, wg, wu, wd, g_final, *, final_norm, tm_e, tf, tm_c):
    t, d = x.shape
    n_e, _, dff = wg.shape
    assert dff % tf == 0 and tm_e % ROWS_PER_DMA_ITER == 0 and t % tm_c == 0
    e_flat = sel[:, 0:TOP_K].astype(jnp.int32).T.reshape(-1)
    g_flat = sel[:, TOP_K:2 * TOP_K].T.reshape(-1)
    tok = jnp.tile(jnp.arange(t, dtype=jnp.int32), TOP_K)
    onehot = (e_flat[:, None] == jnp.arange(n_e, dtype=jnp.int32)[None]).astype(jnp.int32)
    cnt = jnp.sum(onehot, axis=0)
    rank = jnp.take_along_axis(jnp.cumsum(onehot, axis=0) - onehot, e_flat[:, None], axis=1)[:, 0]
    pad_cnt = (cnt + tm_e - 1) // tm_e * tm_e
    pend = jnp.cumsum(pad_cnt)
    pos = (jnp.take(pend - pad_cnt, e_flat) + rank).astype(jnp.int32)
    n_tiles = (TOP_K * t + n_e * (tm_e - 1)) // tm_e
    n_pad = n_tiles * tm_e
    placed = jnp.zeros((n_pad, 2), F32).at[pos].set(jnp.stack([tok.astype(F32), g_flat], axis=1))
    row_src = placed[:, 0].astype(jnp.int32)
    row_gate = placed[:, 1:2]
    tile_e = jnp.minimum(jnp.searchsorted(pend, jnp.arange(n_tiles, dtype=jnp.int32) * tm_e, side="right"),
                         n_e - 1).astype(jnp.int32)
    n_used = (pend[-1] // tm_e).astype(jnp.int32)[None]

    hbm = pl.BlockSpec(memory_space=pl.ANY)
    ys = pl.pallas_call(
        functools.partial(_experts_kernel, tm=tm_e),
        grid_spec=pltpu.PrefetchScalarGridSpec(
            num_scalar_prefetch=3, grid=(n_tiles, dff // tf),
            in_specs=[hbm, pl.BlockSpec((1, d), lambda i, f, *_: (0, 0)),
                      pl.BlockSpec((tm_e, 1), lambda i, f, *_: (i, 0)),
                      pl.BlockSpec((None, d, tf), lambda i, f, src, te, nu: (te[i], 0, f)),
                      pl.BlockSpec((None, d, tf), lambda i, f, src, te, nu: (te[i], 0, f)),
                      pl.BlockSpec((None, tf, d), lambda i, f, src, te, nu: (te[i], f, 0))],
            out_specs=pl.BlockSpec((tm_e, d), lambda i, f, *_: (i, 0)),
            scratch_shapes=[pltpu.VMEM((2, tm_e, d), F32), pltpu.SemaphoreType.DMA((2,)),
                            pltpu.VMEM((tm_e, d), BF16), pltpu.VMEM((tm_e, d), F32)]),
        out_shape=jax.ShapeDtypeStruct((n_pad, d), F32),
        compiler_params=_params(("arbitrary", "arbitrary")), name="experts",
    )(row_src, tile_e, n_used, x, g_ffn, row_gate, wg, wu, wd)

    return pl.pallas_call(
        functools.partial(_combine_kernel, tm=tm_c, t_total=t, final_norm=final_norm),
        grid_spec=pltpu.PrefetchScalarGridSpec(
            num_scalar_prefetch=1, grid=(t // tm_c,),
            in_specs=[pl.BlockSpec((tm_c, d), lambda i, *_: (i, 0)), hbm, pl.BlockSpec((1, d), lambda i, *_: (0, 0))],
            out_specs=pl.BlockSpec((tm_c, d), lambda i, *_: (i, 0)),
            scratch_shapes=[pltpu.VMEM((2, TOP_K, tm_c, d), F32), pltpu.SemaphoreType.DMA((2,))]),
        out_shape=jax.ShapeDtypeStruct((t, d), F32),
        compiler_params=_params(("arbitrary",)), name="moe_combine",
    )(pos, x, ys, g_final)


def _rot_cols(w):
    return jnp.concatenate([-w[..., ROPE_HALF:], w[..., :ROPE_HALF]], axis=-1)


def _pad_cols(w, n):
    return jnp.pad(w, ((0, 0), (0, n - w.shape[1])))


def _layer_weights(l, P):
    d = P["w_in"].shape[1]
    cq_n = P["mla_q_norm"].shape[1]
    ckv_n = P["mla_kv_norm"].shape[1]
    heads, nope = P["mla_w_uk"].shape[2], P["mla_w_uk"].shape[3]
    rope_n = P["mla_w_uq"].shape[3] - nope
    v_n = P["mla_w_uv"].shape[3]
    dh = P["diff_lambda_q1"].shape[1]
    n_groups, chunk = P["chunk_w_s"].shape[1], P["chunk_w_s"].shape[2]
    cw = P["chunk_ln_g"].shape[1]
    assert (cq_n, ckv_n, heads, rope_n, dh, n_groups, chunk, cw) == (256, 128, HEADS, 32, 32, 4, CHUNK, 256)
    dq_n, dk_n, dv_n = 256, 128, 128
    assert P["w_in"].shape[2] == cq_n + ckv_n + rope_n + dq_n + dk_n + dv_n + 2 * cw
    w_in = P["w_in"][l]
    o = 0
    parts = {}
    for name, n in (("cq", cq_n), ("ckv", ckv_n), ("kr", rope_n), ("dq", dq_n), ("dk", dk_n), ("dv", dv_n),
                    ("cu", cw), ("cv", cw)):
        parts[name] = w_in[:, o:o + n]
        o += n
    rot32 = lambda w: _rot_cols(w.reshape(d, -1, 2 * ROPE_HALF)).reshape(d, -1)
    win = jnp.concatenate([
        parts["cq"], parts["ckv"], _pad_cols(parts["kr"], LANES), _pad_cols(rot32(parts["kr"]), LANES),
        parts["dq"], rot32(parts["dq"]), parts["dk"], rot32(parts["dk"]), parts["dv"], parts["cu"], parts["cv"],
    ], axis=1).astype(BF16)

    uq = jnp.transpose(P["mla_w_uq"][l], (1, 0, 2))
    uk = jnp.transpose(P["mla_w_uk"][l], (1, 0, 2))
    wq_lat = _fold_q(uq[:, :, :nope], uk)
    uq_r = uq[:, :, nope:]
    pad_r = lambda w: jnp.transpose(jnp.pad(w, ((0, 0), (0, 0), (0, LANES - rope_n))), (1, 0, 2)).reshape(cq_n, -1)
    wq = jnp.concatenate([wq_lat, pad_r(uq_r), pad_r(_rot_cols(uq_r))], axis=1).astype(BF16)

    uv = jnp.transpose(P["mla_w_uv"][l], (1, 0, 2))
    wuv_pad = jnp.stack([jnp.pad(uv[h], ((0, 0), (h * v_n, (heads - 1 - h) * v_n))) for h in range(heads)])

    ws = P["chunk_w_s"][l]
    bs = P["chunk_b_s"][l]
    rep = lambda a: jnp.repeat(a, cw // n_groups, axis=-1)
    return dict(
        win=win, wq=wq, wuv_pad=wuv_pad.astype(BF16),
        gmix=P["norm_mix"][l][None], gq=P["mla_q_norm"][l][None], gkv=P["mla_kv_norm"][l][None],
        lng=P["chunk_ln_g"][l][None], lnb=P["chunk_ln_b"][l][None],
        ws=ws, bs_full=rep(jnp.transpose(bs)),
        ws_row0=rep(ws[:, 0, 0][None]), bs_row0=rep(bs[:, 0][None]),
        lam=jnp.stack([P["diff_lambda_q1"][l], P["diff_lambda_k1"][l], P["diff_lambda_q2"][l],
                       P["diff_lambda_k2"][l]]),
        subln=jnp.tile(P["diff_subln"][l], 2)[None],
        w_out=P["w_out"][l].astype(BF16), gffn=P["norm_ffn"][l][None],
        mla_scale=float((nope + rope_n) ** -0.5), diff_scale=float(dh ** -0.5),
    )


def _rope_tables(pos):
    inv = 1.0 / (ROPE_THETA ** (jnp.arange(ROPE_HALF, dtype=F32) / ROPE_HALF))
    ang = pos.astype(F32)[:, None] * inv[None, :]
    reps = LANES // ROPE_HALF
    return jnp.tile(jnp.cos(ang), (1, reps)), jnp.tile(jnp.sin(ang), (1, reps))


def kernel(x_prompt, x_sample, cache_mla_latent, cache_mla_krope, cache_diff_k, cache_diff_v, page_table, norm_mix, w_in, mla_q_norm, mla_w_uq, mla_kv_norm, mla_w_uk, mla_w_uv, diff_lambda_q1, diff_lambda_k1, diff_lambda_q2, diff_lambda_k2, diff_subln, chunk_ln_g, chunk_ln_b, chunk_w_s, chunk_b_s, w_out, norm_ffn, ffn_w_gate, ffn_w_up, ffn_w_down, moe_w_router, moe_w_gate, moe_w_up, moe_w_down, norm_final):
    P = dict(norm_mix=norm_mix, w_in=w_in, mla_q_norm=mla_q_norm, mla_w_uq=mla_w_uq, mla_kv_norm=mla_kv_norm,
             mla_w_uk=mla_w_uk, mla_w_uv=mla_w_uv, diff_lambda_q1=diff_lambda_q1, diff_lambda_k1=diff_lambda_k1,
             diff_lambda_q2=diff_lambda_q2, diff_lambda_k2=diff_lambda_k2, diff_subln=diff_subln,
             chunk_ln_g=chunk_ln_g, chunk_ln_b=chunk_ln_b, chunk_w_s=chunk_w_s, chunk_b_s=chunk_b_s,
             w_out=w_out, norm_ffn=norm_ffn)
    depth = w_in.shape[0]
    bp, s, d = x_prompt.shape
    bs, ts, _ = x_sample.shape
    assert bp == 1 and ts == 1
    n_pool, page = cache_mla_latent.shape[1], cache_mla_latent.shape[2]
    past_len = page_table.shape[1] * page
    assert past_len % CHUNK == 0
    n_experts = moe_w_router.shape[2]

    cos_p, sin_p = _rope_tables(jnp.arange(s))
    cos_s, sin_s = _rope_tables(jnp.full((bs,), past_len))
    c_krt = jnp.transpose(cache_mla_krope, (0, 1, 3, 2))
    c_dkt = jnp.transpose(cache_diff_k, (0, 1, 3, 4, 5, 2)).reshape(depth, n_pool, LANES, page)
    c_dvt = jnp.transpose(cache_diff_v, (0, 1, 3, 4, 2)).reshape(depth, n_pool, LANES, page)
    gfin = norm_final[None]

    xp = x_prompt.reshape(s, d)
    xs = x_sample.reshape(bs, d)
    st_p, st_s = [], []
    for l in range(depth):
        w = _layer_weights(l, P)
        lam_init = 0.8 - 0.6 * math.exp(-0.3 * l)
        last = l == depth - 1
        j = l // 2
        if l % 2 == 0:
            mix_w = (ffn_w_gate[j].astype(BF16), ffn_w_up[j].astype(BF16), ffn_w_down[j].astype(BF16))
        else:
            mix_w = (moe_w_gate[j].astype(BF16), moe_w_up[j].astype(BF16), moe_w_down[j].astype(BF16))
            wr_pad = _pad_cols(moe_w_router[j], LANES)

        def channel_mix(x, h, tm):
            if l % 2 == 0:
                return _ffn(x, h, *mix_w, gfin, final_norm=last, tm=tm, tf=mix_w[0].shape[1] // 2)
            gate, sel = _router(x, w["gffn"], wr_pad, n_experts, tm=tm)
            tf = mix_w[0].shape[2] // 2
            tm_e = 512
            if x.shape[0] * TOP_K < n_experts * tm_e:
                return _moe(x, h, gate, *mix_w, gfin, final_norm=last, tm=tm, tf=tf)
            return _moe_routed(x, sel, w["gffn"], *mix_w, gfin, final_norm=last, tm_e=tm_e, tf=tf, tm_c=256)

        qm, kc, qd, kd, vd, oc, ckv, kr, dk, dv, vst = _prep(xs, cos_s, sin_s, w, sample_mode=True, tm=128)
        om, od = _decode(l, page_table, jnp.transpose(qm, (1, 0, 2)), jnp.transpose(qd, (1, 0, 2)),
                         kc[:, None], kd[:, None], vd[:, None],
                         cache_mla_latent, c_krt, c_dkt, c_dvt, cp=32)
        o_mla, o_diff = _sample_epilogue(jnp.transpose(om, (1, 0, 2)), jnp.transpose(od, (1, 0, 2)), w, lam_init)
        xs, hs = _outproj(xs, o_mla, o_diff, oc, w["w_out"], w["gffn"], tm=128)
        xs = channel_mix(xs, hs, 128)
        st_s.append((ckv, kr, dk, dv, vst))

        qm, kc, qd, kd, _, oc, ckv, kr, dk, dv, vme, vde = _prep(xp, cos_p, sin_p, w, sample_mode=False, tm=256)
        o_mla = _flash(qm, kc, vme, (w["wuv_pad"],), mode="mla", lam_init=lam_init, tq=512, tk=1024, groups=8)
        o_diff = _flash(qd, kd, vde, (w["lam"], w["subln"]), mode="diff", lam_init=lam_init, tq=512, tk=2048,
                        groups=8)
        xp, hp = _outproj(xp, o_mla, o_diff, oc, w["w_out"], w["gffn"], tm=512)
        xp = channel_mix(xp, hp, 512)
        st_p.append((ckv, kr, dk, dv))

    stack = lambda sts, i, shape: jnp.stack([st[i] for st in sts], axis=0).reshape(shape)
    return (xp.reshape(bp, s, d), xs.reshape(bs, ts, d),
            stack(st_p, 0, (depth, bp, s, -1)), stack(st_p, 1, (depth, bp, s, -1)),
            stack(st_p, 2, (depth, bp, s) + cache_diff_k.shape[3:]),
            stack(st_p, 3, (depth, bp, s) + cache_diff_v.shape[3:]),
            stack(st_s, 0, (depth, bs, ts, -1)), stack(st_s, 1, (depth, bs, ts, -1)),
            stack(st_s, 2, (depth, bs, ts) + cache_diff_k.shape[3:]),
            stack(st_s, 3, (depth, bs, ts) + cache_diff_v.shape[3:]),
            stack(st_s, 4, (depth, bs, ts, -1)))
```

```python
import functools
import math

import jax
import jax.numpy as jnp
from jax import lax
from jax.experimental import pallas as pl
from jax.experimental.pallas import tpu as pltpu

F32 = jnp.float32
BF16 = jnp.bfloat16

ROPE_THETA = 10000.0
NORM_EPS = 1e-6
LN_EPS = 1e-5
NEG_INF = -1e30
TOP_K = 2
LOG2E = math.log2(math.e)

LANES = 128
HEADS = 8
ROPE_HALF = 16
CHUNK = 128
VMEM_LIMIT = 56 * 1024 * 1024
DECODE_SLOTS = 3


def _params(sem):
    return pltpu.CompilerParams(dimension_semantics=sem, vmem_limit_bytes=VMEM_LIMIT)


def _rms(x, g, eps):
    return x * lax.rsqrt(jnp.mean(x * x, axis=-1, keepdims=True) + eps) * g


def _gelu(x):
    return 0.5 * x * (1.0 + jnp.tanh(math.sqrt(2.0 / math.pi) * (x + 0.044715 * (x * x * x))))


def _full(shape):
    return pl.BlockSpec(shape, lambda *_: (0,) * len(shape))


def _fold_q_kernel(uq_ref, uk_ref, o_ref):
    for h in range(HEADS):
        o_ref[:, h * LANES:(h + 1) * LANES] = lax.dot_general(
            uq_ref[h], uk_ref[h], (((1,), (1,)), ((), ())),
            precision=lax.Precision.HIGHEST, preferred_element_type=F32)


def _fold_q(uq_nope, uk):
    h, cq, _ = uq_nope.shape
    return pl.pallas_call(
        _fold_q_kernel,
        out_shape=jax.ShapeDtypeStruct((cq, h * uk.shape[1]), F32),
        name="fold_q",
    )(uq_nope, uk)


def _prep_kernel(x_ref, gmix_ref, win_ref, gq_ref, wq_ref, gkv_ref, cos_ref, sin_ref, lng_ref, lnb_ref,
                 ws_ref, bs_ref,
                 qm_ref, kc_ref, qd_ref, kd_ref, vd_ref, oc_ref, ckv_ref, kr_ref, dk_ref, dv_ref, *rest,
                 sample_mode, mla_scale, diff_scale):
    tm = x_ref.shape[0]
    x = x_ref[...]
    xn = _rms(x, gmix_ref[...], NORM_EPS).astype(BF16)
    z = jnp.dot(xn, win_ref[...], preferred_element_type=F32)
    cos = cos_ref[...]
    sin = sin_ref[...]
    col = lambda a, n=1: z[:, a * LANES:(a + n) * LANES]

    cqn = _rms(col(0, 2), gq_ref[...], NORM_EPS).astype(BF16)
    qall = jnp.dot(cqn, wq_ref[...], preferred_element_type=F32)
    for h in range(HEADS):
        qlat = qall[:, h * LANES:(h + 1) * LANES]
        qa = qall[:, (HEADS + h) * LANES:(HEADS + h + 1) * LANES]
        qb = qall[:, (2 * HEADS + h) * LANES:(2 * HEADS + h + 1) * LANES]
        qm_ref[h, :, 0:LANES] = (qlat * mla_scale).astype(BF16)
        qm_ref[h, :, LANES:2 * LANES] = ((qa * cos + qb * sin) * mla_scale).astype(BF16)
    ckvn = _rms(col(2), gkv_ref[...], NORM_EPS)
    kr = col(3) * cos + col(4) * sin
    ckv_ref[...] = ckvn
    kr_ref[...] = kr[:, 0:2 * ROPE_HALF]
    kc_ref[:, 0:LANES] = ckvn.astype(BF16)
    kc_ref[:, LANES:2 * LANES] = kr.astype(BF16)

    lane = lax.broadcasted_iota(jnp.int32, (1, LANES), 1)
    for g in range(2):
        chunk = (col(5 + g) * cos + col(7 + g) * sin) * diff_scale
        rolled = pltpu.roll(chunk, LANES // 2, axis=1)
        for r in range(2):
            src = chunk if r == g else rolled
            for m in range(2):
                lo = g * 64 + m * 32
                sel = (lane >= lo) & (lane < lo + 32)
                qd_ref[g * 4 + r * 2 + m] = jnp.where(sel, src, 0.0).astype(BF16)
    dk = col(9) * cos + col(10) * sin
    dv = col(11)
    dk_ref[...] = dk
    dv_ref[...] = dv
    kd_ref[...] = dk.astype(BF16)
    vd_ref[...] = dv.astype(BF16)

    u = _gelu(col(12, 2))
    gv = _gelu(col(14, 2))
    mu = jnp.mean(gv, axis=-1, keepdims=True)
    gc = gv - mu
    var = jnp.mean(gc * gc, axis=-1, keepdims=True)
    v = gc * lax.rsqrt(var + LN_EPS) * lng_ref[...] + lnb_ref[...]
    if sample_mode:
        rest[0][...] = v
        oc_ref[...] = (u * (v * ws_ref[...] + bs_ref[...])).astype(BF16)
    else:
        vme_ref, vde_ref = rest
        ones_col = jnp.broadcast_to(jnp.where(lane == 0, 1.0, 0.0), (tm, LANES)).astype(BF16)
        vme_ref[:, 0:LANES] = ckvn.astype(BF16)
        vme_ref[:, LANES:2 * LANES] = ones_col
        vde_ref[:, 0:LANES] = dv.astype(BF16)
        vde_ref[:, LANES:2 * LANES] = ones_col
        lane2 = lax.broadcasted_iota(jnp.int32, (1, 2 * LANES), 1)
        row = lax.broadcasted_iota(jnp.int32, (CHUNK, CHUNK), 0)
        cl = lax.broadcasted_iota(jnp.int32, (CHUNK, CHUNK), 1)
        wts = [jnp.where(row >= cl, ws_ref[g], 0.0).astype(BF16) for g in range(4)]
        for c in range(tm // CHUNK):
            vc = v[c * CHUNK:(c + 1) * CHUNK]
            mix = bs_ref[...]
            for g in range(4):
                vg = jnp.where((lane2 >= g * 64) & (lane2 < (g + 1) * 64), vc, 0.0).astype(BF16)
                mix = mix + jnp.dot(wts[g], vg, preferred_element_type=F32)
            oc_ref[c * CHUNK:(c + 1) * CHUNK, :] = (u[c * CHUNK:(c + 1) * CHUNK] * mix).astype(BF16)


def _prep(x, cos, sin, w, *, sample_mode, tm):
    t, d = x.shape
    tm = min(tm, t)
    nin = w["win"].shape[1]
    nq = w["wq"].shape[1]
    tok = lambda n: pl.BlockSpec((tm, n), lambda i: (i, 0))
    heads = lambda n: pl.BlockSpec((HEADS, tm, n), lambda i: (0, i, 0))
    ws, bs = (w["ws_row0"], w["bs_row0"]) if sample_mode else (w["ws"], w["bs_full"])
    in_specs = [tok(d), _full((1, d)), _full((d, nin)), _full((1, 2 * LANES)), _full((2 * LANES, nq)),
                _full((1, LANES)), tok(LANES), tok(LANES), _full((1, 2 * LANES)), _full((1, 2 * LANES)),
                _full(ws.shape), _full(bs.shape)]
    out_shape = [jax.ShapeDtypeStruct((HEADS, t, 2 * LANES), BF16),
                 jax.ShapeDtypeStruct((t, 2 * LANES), BF16),
                 jax.ShapeDtypeStruct((HEADS, t, LANES), BF16),
                 jax.ShapeDtypeStruct((t, LANES), BF16),
                 jax.ShapeDtypeStruct((t, LANES), BF16),
                 jax.ShapeDtypeStruct((t, 2 * LANES), BF16),
                 jax.ShapeDtypeStruct((t, LANES), F32),
                 jax.ShapeDtypeStruct((t, 2 * ROPE_HALF), F32),
                 jax.ShapeDtypeStruct((t, LANES), F32),
                 jax.ShapeDtypeStruct((t, LANES), F32)]
    out_specs = [heads(2 * LANES), tok(2 * LANES), heads(LANES), tok(LANES), tok(LANES), tok(2 * LANES),
                 tok(LANES), tok(2 * ROPE_HALF), tok(LANES), tok(LANES)]
    if sample_mode:
        out_shape.append(jax.ShapeDtypeStruct((t, 2 * LANES), F32))
        out_specs.append(tok(2 * LANES))
    else:
        out_shape += [jax.ShapeDtypeStruct((t, 2 * LANES), BF16)] * 2
        out_specs += [tok(2 * LANES)] * 2
    kern = functools.partial(_prep_kernel, sample_mode=sample_mode, mla_scale=w["mla_scale"] * LOG2E,
                             diff_scale=w["diff_scale"] * LOG2E)
    return pl.pallas_call(
        kern, grid=(t // tm,), in_specs=in_specs, out_specs=out_specs, out_shape=out_shape,
        compiler_params=_params(("parallel",)), name="prep_sample" if sample_mode else "prep_prompt",
    )(x, w["gmix"], w["win"], w["gq"], w["wq"], w["gkv"], cos, sin, w["lng"], w["lnb"], ws, bs)


def _mla_epilogue(o, wuv_ref):
    out = None
    for h in range(HEADS):
        part = jnp.dot(o[h].astype(BF16), wuv_ref[h], preferred_element_type=F32)
        out = part if out is None else out + part
    return out


def _diff_lambda(lam_ref, lam_init):
    p = lam_ref[...]
    s1 = jnp.sum(p[0:1] * p[1:2], axis=-1, keepdims=True)
    s2 = jnp.sum(p[2:3] * p[3:4], axis=-1, keepdims=True)
    return jnp.exp(s1) - jnp.exp(s2) + lam_init


def _diff_epilogue(o, lam, subln, lam_init):
    lane = lax.broadcasted_iota(jnp.int32, (1, LANES), 1)
    chunks = []
    for g in range(2):
        valid = (lane >= g * 64) & (lane < (g + 1) * 64)
        halves = []
        for r in range(2):
            d = o[g * 4 + r * 2] - lam * o[g * 4 + r * 2 + 1]
            ms = jnp.sum(jnp.where(valid, d * d, 0.0), axis=-1, keepdims=True) * (1.0 / 64.0)
            y = d * lax.rsqrt(ms + LN_EPS) * subln * (1.0 - lam_init)
            halves.append(y if r == g else pltpu.roll(y, LANES // 2, axis=1))
        chunks.append(jnp.where(lane < 64, halves[0], halves[1]))
    return chunks


def _flash_kernel(q_ref, kt_ref, v_ref, *rest, tq, tk, groups, mode, lam_init):
    if mode == "mla":
        wuv_ref, out_ref, m_ref, acc_ref = rest
    else:
        lam_ref, subln_ref, out_ref, m_ref, acc_ref = rest
    i = pl.program_id(0)
    hg = HEADS // groups
    rows = hg * tq
    m_ref[...] = jnp.full(m_ref.shape, NEG_INF, F32)
    acc_ref[...] = jnp.zeros(acc_ref.shape, F32)

    def step(j, masked):
        kt = kt_ref[j]
        v = v_ref[pl.ds(pl.multiple_of(j * tk, tk), tk), :]
        if masked:
            kpos = j * tk + lax.broadcasted_iota(jnp.int32, (1, tk), 1)
            qpos = i * tq + (lax.broadcasted_iota(jnp.int32, (rows, 1), 0) & (tq - 1))
            keep = kpos <= qpos
        for g in range(groups):
            q = q_ref[g * hg:(g + 1) * hg].reshape(rows, q_ref.shape[2])
            s = jnp.dot(q, kt, preferred_element_type=F32)
            if masked:
                s = jnp.where(keep, s, NEG_INF)
            m_prev = m_ref[g]
            m_new = jnp.maximum(m_prev, jnp.max(s, axis=-1, keepdims=True))
            alpha = jnp.exp2(m_prev - m_new)
            if mode == "mla":
                p = jnp.exp2((s - m_new).astype(BF16))
            else:
                p = jnp.exp2(s - m_new).astype(BF16)
            acc_ref[g] = alpha * acc_ref[g] + jnp.dot(p, v, preferred_element_type=F32)
            m_ref[g] = m_new

    n_full = (i * tq) // tk

    def body(j, carry):
        step(j, False)
        return carry

    lax.fori_loop(0, n_full, body, 0)
    step(n_full, True)

    o = []
    for g in range(groups):
        acc = acc_ref[g]
        og = acc[:, 0:LANES] / acc[:, LANES:LANES + 1]
        o += [og[h * tq:(h + 1) * tq] for h in range(hg)]
    if mode == "mla":
        out_ref[...] = _mla_epilogue(o, wuv_ref).astype(BF16)
    else:
        chunks = _diff_epilogue(o, _diff_lambda(lam_ref, lam_init), subln_ref[...], lam_init)
        out_ref[:, 0:LANES] = chunks[0].astype(BF16)
        out_ref[:, LANES:2 * LANES] = chunks[1].astype(BF16)


def _flash(q, k, v_ext, extra, *, mode, lam_init, tq, tk, groups):
    _, s, dk = q.shape
    tq, tk = min(tq, s), min(tk, s)
    assert tk % tq == 0 and s % tk == 0 and tq & (tq - 1) == 0 and HEADS % groups == 0
    kt = jnp.transpose(k.reshape(s // tk, tk, dk), (0, 2, 1))
    in_specs = [pl.BlockSpec((HEADS, tq, dk), lambda i: (0, i, 0)), _full(kt.shape), _full(v_ext.shape)]
    in_specs += [_full(e.shape) for e in extra]
    nout = 512 if mode == "mla" else 2 * LANES
    rows = HEADS // groups * tq
    kern = functools.partial(_flash_kernel, tq=tq, tk=tk, groups=groups, mode=mode, lam_init=lam_init)
    return pl.pallas_call(
        kern, grid=(s // tq,), in_specs=in_specs,
        out_specs=pl.BlockSpec((tq, nout), lambda i: (i, 0)),
        out_shape=jax.ShapeDtypeStruct((s, nout), BF16),
        scratch_shapes=[pltpu.VMEM((groups, rows, 1), F32), pltpu.VMEM((groups, rows, 2 * LANES), F32)],
        compiler_params=_params(("parallel",)), name="flash_" + mode,
    )(q, kt, v_ext, *extra)


def _decode_kernel(pt_ref, qm_ref, qd_ref, knm_ref, knd_ref, vnd_ref, lat_hbm, kr_hbm, dk_hbm, dv_hbm,
                   om_ref, od_ref, lat_buf, kr_buf, dk_buf, dv_buf, sem, m_ref, l_ref, acc_ref,
                   *, layer, n_chunks, n_steps, cp, page):
    t = pl.program_id(0)
    c = t % n_chunks
    n_slots = lat_buf.shape[0]
    ahead = n_slots - 1
    slot = t % n_slots

    def chunk_copies(step, slot):
        b = step // n_chunks
        first = (step % n_chunks) * cp
        out = []
        for i in range(cp):
            pg = pt_ref[b, first + i]
            tok = pl.ds(i * page, page)
            out.append(pltpu.make_async_copy(lat_hbm.at[layer, pg], lat_buf.at[slot, tok, :], sem.at[slot]))
            out.append(pltpu.make_async_copy(kr_hbm.at[layer, pg], kr_buf.at[slot, :, tok], sem.at[slot]))
            out.append(pltpu.make_async_copy(dk_hbm.at[layer, pg], dk_buf.at[slot, :, tok], sem.at[slot]))
            out.append(pltpu.make_async_copy(dv_hbm.at[layer, pg], dv_buf.at[slot, :, tok], sem.at[slot]))
        return out

    @pl.when(t == 0)
    def _():
        for first in range(min(ahead, n_steps)):
            for cpy in chunk_copies(first, first):
                cpy.start()

    @pl.when(t + ahead < n_steps)
    def _():
        for cpy in chunk_copies(t + ahead, (t + ahead) % n_slots):
            cpy.start()

    for cpy in chunk_copies(t, slot):
        cpy.wait()

    @pl.when(c == 0)
    def _():
        m_ref[...] = jnp.full(m_ref.shape, NEG_INF, F32)
        l_ref[...] = jnp.zeros(l_ref.shape, F32)
        acc_ref[...] = jnp.zeros(acc_ref.shape, F32)

    nt = (((1,), (1,)), ((), ()))
    qm = qm_ref[...]
    qd = qd_ref[...]
    lat = lat_buf[slot].astype(BF16)
    s_m = (lax.dot_general(qm[:, 0:LANES], lat, nt, preferred_element_type=F32)
           + jnp.dot(qm[:, LANES:LANES + 2 * ROPE_HALF], kr_buf[slot].astype(BF16), preferred_element_type=F32))
    s_d = jnp.dot(qd, dk_buf[slot].astype(BF16), preferred_element_type=F32)

    def update(a, s, pv):
        m_prev = m_ref[a]
        m_new = jnp.maximum(m_prev, jnp.max(s, axis=-1, keepdims=True))
        alpha = jnp.exp2(m_prev - m_new)
        pr = jnp.exp2(s - m_new)
        l_ref[a] = alpha * l_ref[a] + jnp.sum(pr, axis=-1, keepdims=True)
        acc_ref[a] = alpha * acc_ref[a] + pv(pr.astype(BF16))
        m_ref[a] = m_new

    update(0, s_m, lambda pr: jnp.dot(pr, lat, preferred_element_type=F32))
    update(1, s_d, lambda pr: lax.dot_general(pr, dv_buf[slot].astype(BF16), nt, preferred_element_type=F32))

    @pl.when(c == n_chunks - 1)
    def _():
        def finish(a, q, kn, vn, o_ref):
            s = jnp.sum(q.astype(F32) * kn.astype(F32), axis=-1, keepdims=True)
            m_prev = m_ref[a]
            m_new = jnp.maximum(m_prev, s)
            alpha = jnp.exp2(m_prev - m_new)
            pr = jnp.exp2(s - m_new)
            l = alpha * l_ref[a] + pr
            acc = alpha * acc_ref[a] + pr * vn.astype(F32)
            o_ref[...] = acc / l

        finish(0, qm, knm_ref[...], knm_ref[:, 0:LANES], om_ref)
        finish(1, qd, knd_ref[...], vnd_ref[...], od_ref)


def _decode(layer, page_table, qm, qd, kn_m, kn_d, vn_d, c_lat, c_krt, c_dkt, c_dvt, *, cp):
    b, n_pages = page_table.shape
    page = c_lat.shape[2]
    cp = min(cp, n_pages)
    assert n_pages % cp == 0
    n_chunks = n_pages // cp
    keys = cp * page
    seq = lambda n, w: pl.BlockSpec((None, n, w), lambda t, pt: (t // n_chunks, 0, 0))
    hbm = pl.BlockSpec(memory_space=pl.ANY)
    grid_spec = pltpu.PrefetchScalarGridSpec(
        num_scalar_prefetch=1, grid=(b * n_chunks,),
        in_specs=[seq(HEADS, 2 * LANES), seq(HEADS, LANES), seq(1, 2 * LANES), seq(1, LANES), seq(1, LANES),
                  hbm, hbm, hbm, hbm],
        out_specs=[seq(HEADS, LANES), seq(HEADS, LANES)],
        scratch_shapes=[pltpu.VMEM((DECODE_SLOTS, keys, LANES), F32),
                        pltpu.VMEM((DECODE_SLOTS, c_krt.shape[2], keys), F32),
                        pltpu.VMEM((DECODE_SLOTS, LANES, keys), F32), pltpu.VMEM((DECODE_SLOTS, LANES, keys), F32),
                        pltpu.SemaphoreType.DMA((DECODE_SLOTS,)),
                        pltpu.VMEM((2, HEADS, 1), F32), pltpu.VMEM((2, HEADS, 1), F32),
                        pltpu.VMEM((2, HEADS, LANES), F32)])
    kern = functools.partial(_decode_kernel, layer=layer, n_chunks=n_chunks, n_steps=b * n_chunks, cp=cp, page=page)
    return pl.pallas_call(
        kern, grid_spec=grid_spec,
        out_shape=[jax.ShapeDtypeStruct((b, HEADS, LANES), F32), jax.ShapeDtypeStruct((b, HEADS, LANES), F32)],
        compiler_params=_params(("arbitrary",)), name="decode",
    )(page_table, qm, qd, kn_m, kn_d, vn_d, c_lat, c_krt, c_dkt, c_dvt)


def _sample_epilogue_kernel(om_ref, od_ref, wuv_ref, lam_ref, subln_ref, omla_ref, odiff_ref, *, lam_init):
    om = [om_ref[h] for h in range(HEADS)]
    od = [od_ref[h] for h in range(HEADS)]
    omla_ref[...] = _mla_epilogue(om, wuv_ref).astype(BF16)
    chunks = _diff_epilogue(od, _diff_lambda(lam_ref, lam_init), subln_ref[...], lam_init)
    odiff_ref[:, 0:LANES] = chunks[0].astype(BF16)
    odiff_ref[:, LANES:2 * LANES] = chunks[1].astype(BF16)


def _sample_epilogue(om, od, w, lam_init):
    t = om.shape[1]
    return pl.pallas_call(
        functools.partial(_sample_epilogue_kernel, lam_init=lam_init),
        out_shape=[jax.ShapeDtypeStruct((t, 512), BF16), jax.ShapeDtypeStruct((t, 2 * LANES), BF16)],
        name="sample_epilogue",
    )(om, od, w["wuv_pad"], w["lam"], w["subln"])


def _outproj_kernel(x_ref, a_ref, b_ref, c_ref, w_ref, g_ref, xo_ref, h_ref):
    na, nb = a_ref.shape[1], b_ref.shape[1]
    y = (jnp.dot(a_ref[...], w_ref[0:na, :], preferred_element_type=F32)
         + jnp.dot(b_ref[...], w_ref[na:na + nb, :], preferred_element_type=F32)
         + jnp.dot(c_ref[...], w_ref[na + nb:, :], preferred_element_type=F32))
    xo = x_ref[...] + y
    xo_ref[...] = xo
    h_ref[...] = _rms(xo, g_ref[...], NORM_EPS).astype(BF16)


def _outproj(x, a, b, c, w_out, g, *, tm):
    t, d = x.shape
    tm = min(tm, t)
    tok = lambda n: pl.BlockSpec((tm, n), lambda i: (i, 0))
    return pl.pallas_call(
        _outproj_kernel, grid=(t // tm,),
        in_specs=[tok(d), tok(a.shape[1]), tok(b.shape[1]), tok(c.shape[1]), _full(w_out.shape), _full((1, d))],
        out_specs=[tok(d), tok(d)],
        out_shape=[jax.ShapeDtypeStruct((t, d), F32), jax.ShapeDtypeStruct((t, d), BF16)],
        compiler_params=_params(("parallel",)), name="outproj",
    )(x, a, b, c, w_out, g)


def _swiglu_tile(h, wg, wu, wd):
    g = jnp.dot(h, wg, preferred_element_type=F32)
    u = jnp.dot(h, wu, preferred_element_type=F32)
    act = (g * jax.nn.sigmoid(g) * u).astype(BF16)
    return jnp.dot(act, wd, preferred_element_type=F32)


def _ffn_kernel(x_ref, h_ref, wg_ref, wu_ref, wd_ref, gf_ref, o_ref, acc_ref, *, final_norm):
    f = pl.program_id(1)

    @pl.when(f == 0)
    def _():
        acc_ref[...] = jnp.zeros(acc_ref.shape, F32)

    acc_ref[...] += _swiglu_tile(h_ref[...], wg_ref[...], wu_ref[...], wd_ref[...])

    @pl.when(f == pl.num_programs(1) - 1)
    def _():
        y = x_ref[...] + acc_ref[...]
        o_ref[...] = _rms(y, gf_ref[...], NORM_EPS) if final_norm else y


def _ffn(x, h, wg, wu, wd, g_final, *, final_norm, tm, tf):
    t, d = x.shape
    dff = wg.shape[1]
    tm = min(tm, t)
    assert dff % tf == 0
    return pl.pallas_call(
        functools.partial(_ffn_kernel, final_norm=final_norm), grid=(t // tm, dff // tf),
        in_specs=[pl.BlockSpec((tm, d), lambda i, f: (i, 0)), pl.BlockSpec((tm, d), lambda i, f: (i, 0)),
                  pl.BlockSpec((d, tf), lambda i, f: (0, f)), pl.BlockSpec((d, tf), lambda i, f: (0, f)),
                  pl.BlockSpec((tf, d), lambda i, f: (f, 0)), pl.BlockSpec((1, d), lambda i, f: (0, 0))],
        out_specs=pl.BlockSpec((tm, d), lambda i, f: (i, 0)),
        out_shape=jax.ShapeDtypeStruct((t, d), F32),
        scratch_shapes=[pltpu.VMEM((tm, d), F32)],
        compiler_params=_params(("parallel", "arbitrary")), name="ffn",
    )(x, h, wg, wu, wd, g_final)


def _router_kernel(x_ref, g_ref, wr_ref, gate_ref, sel_ref, *, n_experts):
    h = _rms(x_ref[...], g_ref[...], NORM_EPS)
    logits = jnp.dot(h, wr_ref[...], precision=lax.Precision.HIGHEST, preferred_element_type=F32)
    lane = lax.broadcasted_iota(jnp.int32, logits.shape, 1).astype(F32)
    logits = jnp.where(lane < n_experts, logits, NEG_INF)
    gate = jnp.zeros(logits.shape, F32)
    tops, picks = [], []
    for _ in range(TOP_K):
        top = jnp.max(logits, axis=-1, keepdims=True)
        pick = jnp.min(jnp.where(logits == top, lane, float(LANES)), axis=-1, keepdims=True)
        tops.append(top)
        picks.append(pick)
        logits = jnp.where(lane == pick, NEG_INF, logits)
    e = [jnp.exp(tv - tops[0]) for tv in tops]
    den = e[0]
    for ev in e[1:]:
        den = den + ev
    sel = jnp.zeros(logits.shape, F32)
    for k, (ev, pick) in enumerate(zip(e, picks)):
        gk = ev / den
        gate = gate + jnp.where(lane == pick, gk, 0.0)
        sel = sel + jnp.where(lane == k, pick, 0.0) + jnp.where(lane == TOP_K + k, gk, 0.0)
    gate_ref[...] = gate
    sel_ref[...] = sel


def _router(x, g, wr_pad, n_experts, *, tm):
    t, d = x.shape
    tm = min(tm, t)
    return pl.pallas_call(
        functools.partial(_router_kernel, n_experts=n_experts), grid=(t // tm,),
        in_specs=[pl.BlockSpec((tm, d), lambda i: (i, 0)), _full((1, d)), _full(wr_pad.shape)],
        out_specs=[pl.BlockSpec((tm, LANES), lambda i: (i, 0))] * 2,
        out_shape=[jax.ShapeDtypeStruct((t, LANES), F32)] * 2,
        compiler_params=_params(("parallel",)), name="router",
    )(x, g, wr_pad)


def _moe_kernel(x_ref, h_ref, gate_ref, wg_ref, wu_ref, wd_ref, gf_ref, o_ref, acc_ref, *, final_norm):
    e = pl.program_id(1)
    f = pl.program_id(2)

    @pl.when((e == 0) & (f == 0))
    def _():
        acc_ref[...] = jnp.zeros(acc_ref.shape, F32)

    lane = lax.broadcasted_iota(jnp.int32, (1, LANES), 1)
    ge = jnp.sum(jnp.where(lane == e, gate_ref[...], 0.0), axis=-1, keepdims=True)
    acc_ref[...] += ge * _swiglu_tile(h_ref[...], wg_ref[...], wu_ref[...], wd_ref[...])

    @pl.when((e == pl.num_programs(1) - 1) & (f == pl.num_programs(2) - 1))
    def _():
        y = x_ref[...] + acc_ref[...]
        o_ref[...] = _rms(y, gf_ref[...], NORM_EPS) if final_norm else y


def _moe(x, h, gate, wg, wu, wd, g_final, *, final_norm, tm, tf):
    t, d = x.shape
    n_e, _, dff = wg.shape
    tm = min(tm, t)
    assert dff % tf == 0
    return pl.pallas_call(
        functools.partial(_moe_kernel, final_norm=final_norm), grid=(t // tm, n_e, dff // tf),
        in_specs=[pl.BlockSpec((tm, d), lambda i, e, f: (i, 0)), pl.BlockSpec((tm, d), lambda i, e, f: (i, 0)),
                  pl.BlockSpec((tm, LANES), lambda i, e, f: (i, 0)),
                  pl.BlockSpec((None, d, tf), lambda i, e, f: (e, 0, f)),
                  pl.BlockSpec((None, d, tf), lambda i, e, f: (e, 0, f)),
                  pl.BlockSpec((None, tf, d), lambda i, e, f: (e, f, 0)),
                  pl.BlockSpec((1, d), lambda i, e, f: (0, 0))],
        out_specs=pl.BlockSpec((tm, d), lambda i, e, f: (i, 0)),
        out_shape=jax.ShapeDtypeStruct((t, d), F32),
        scratch_shapes=[pltpu.VMEM((tm, d), F32)],
        compiler_params=_params(("parallel", "arbitrary", "arbitrary")), name="moe",
    )(x, h, gate, wg, wu, wd, g_final)


ROWS_PER_DMA_ITER = 8


def _row_gather(idx_ref, base, src_hbm, dst_buf, sem, n_rows, op):
    def row_copy(r):
        return pltpu.make_async_copy(src_hbm.at[pl.ds(idx_ref[base + r], 1), :], dst_buf.at[pl.ds(r, 1), :], sem)

    if op == "start":
        for r in range(n_rows):
            row_copy(r).start()
    else:
        def body(it, carry):
            for u in range(ROWS_PER_DMA_ITER):
                row_copy(it * ROWS_PER_DMA_ITER + u).wait()
            return carry

        lax.fori_loop(0, n_rows // ROWS_PER_DMA_ITER, body, 0)


def _experts_kernel(src_ref, te_ref, nused_ref, x_hbm, g_ref, gate_ref, wg_ref, wu_ref, wd_ref, ys_ref,
                    xbuf, sem, hbuf, acc_ref, *, tm):
    i = pl.program_id(0)
    f = pl.program_id(1)
    n_used = nused_ref[0]
    slot = i % 2
    used = i < n_used

    @pl.when((f == 0) & used)
    def _():
        @pl.when(i == 0)
        def _():
            _row_gather(src_ref, 0, x_hbm, xbuf.at[0], sem.at[0], tm, "start")

        @pl.when(i + 1 < n_used)
        def _():
            _row_gather(src_ref, (i + 1) * tm, x_hbm, xbuf.at[1 - slot], sem.at[1 - slot], tm, "start")

        _row_gather(src_ref, i * tm, x_hbm, xbuf.at[slot], sem.at[slot], tm, "wait")
        hbuf[...] = _rms(xbuf[slot], g_ref[...], NORM_EPS).astype(BF16)

    @pl.when(used)
    def _():
        y = _swiglu_tile(hbuf[...], wg_ref[...], wu_ref[...], wd_ref[...])

        @pl.when(f == 0)
        def _():
            acc_ref[...] = y

        @pl.when(f > 0)
        def _():
            acc_ref[...] += y

    @pl.when(f == pl.num_programs(1) - 1)
    def _():
        @pl.when(used)
        def _():
            ys_ref[...] = acc_ref[...] * gate_ref[...]

        @pl.when(jnp.logical_not(used))
        def _():
            ys_ref[...] = jnp.zeros(ys_ref.shape, F32)


def _combine_kernel(pos_ref, x_ref, ys_hbm, gf_ref, o_ref, ybuf, sem, *, tm, t_total, final_norm):
    i = pl.program_id(0)
    slot = i % 2

    def gather(tile, slot, op):
        for k in range(TOP_K):
            _row_gather(pos_ref, k * t_total + tile * tm, ys_hbm, ybuf.at[slot, k], sem.at[slot], tm, op)

    @pl.when(i == 0)
    def _():
        gather(0, 0, "start")

    @pl.when(i + 1 < pl.num_programs(0))
    def _():
        gather(i + 1, 1 - slot, "start")

    gather(i, slot, "wait")
    y = x_ref[...]
    for k in range(TOP_K):
        y = y + ybuf[slot, k]
    o_ref[...] = _rms(y, gf_ref[...], NORM_EPS) if final_norm else y


def _moe_routed(x, sel, g_ffn, wg, wu, wd, g_final, *, final_norm, tm_e, tf, tm_c):
    t, d = x.shape
    n_e, _, dff = wg.shape
    assert dff % tf == 0 and tm_e % ROWS_PER_DMA_ITER == 0 and t % tm_c == 0
    e_flat = sel[:, 0:TOP_K].astype(jnp.int32).T.reshape(-1)
    g_flat = sel[:, TOP_K:2 * TOP_K].T.reshape(-1)
    tok = jnp.tile(jnp.arange(t, dtype=jnp.int32), TOP_K)
    onehot = (e_flat[:, None] == jnp.arange(n_e, dtype=jnp.int32)[None]).astype(jnp.int32)
    cnt = jnp.sum(onehot, axis=0)
    rank = jnp.take_along_axis(jnp.cumsum(onehot, axis=0) - onehot, e_flat[:, None], axis=1)[:, 0]
    pad_cnt = (cnt + tm_e - 1) // tm_e * tm_e
    pend = jnp.cumsum(pad_cnt)
    pos = (jnp.take(pend - pad_cnt, e_flat) + rank).astype(jnp.int32)
    n_tiles = (TOP_K * t + n_e * (tm_e - 1)) // tm_e
    n_pad = n_tiles * tm_e
    placed = jnp.zeros((n_pad, 2), F32).at[pos].set(jnp.stack([tok.astype(F32), g_flat], axis=1))
    row_src = placed[:, 0].astype(jnp.int32)
    row_gate = placed[:, 1:2]
    tile_e = jnp.minimum(jnp.searchsorted(pend, jnp.arange(n_tiles, dtype=jnp.int32) * tm_e, side="right"),
                         n_e - 1).astype(jnp.int32)
    n_used = (pend[-1] // tm_e).astype(jnp.int32)[None]

    hbm = pl.BlockSpec(memory_space=pl.ANY)
    ys = pl.pallas_call(
        functools.partial(_experts_kernel, tm=tm_e),
        grid_spec=pltpu.PrefetchScalarGridSpec(
            num_scalar_prefetch=3, grid=(n_tiles, dff // tf),
            in_specs=[hbm, pl.BlockSpec((1, d), lambda i, f, *_: (0, 0)),
                      pl.BlockSpec((tm_e, 1), lambda i, f, *_: (i, 0)),
                      pl.BlockSpec((None, d, tf), lambda i, f, src, te, nu: (te[i], 0, f)),
                      pl.BlockSpec((None, d, tf), lambda i, f, src, te, nu: (te[i], 0, f)),
                      pl.BlockSpec((None, tf, d), lambda i, f, src, te, nu: (te[i], f, 0))],
            out_specs=pl.BlockSpec((tm_e, d), lambda i, f, *_: (i, 0)),
            scratch_shapes=[pltpu.VMEM((2, tm_e, d), F32), pltpu.SemaphoreType.DMA((2,)),
                            pltpu.VMEM((tm_e, d), BF16), pltpu.VMEM((tm_e, d), F32)]),
        out_shape=jax.ShapeDtypeStruct((n_pad, d), F32),
        compiler_params=_params(("arbitrary", "arbitrary")), name="experts",
    )(row_src, tile_e, n_used, x, g_ffn, row_gate, wg, wu, wd)

    return pl.pallas_call(
        functools.partial(_combine_kernel, tm=tm_c, t_total=t, final_norm=final_norm),
        grid_spec=pltpu.PrefetchScalarGridSpec(
            num_scalar_prefetch=1, grid=(t // tm_c,),
            in_specs=[pl.BlockSpec((tm_c, d), lambda i, *_: (i, 0)), hbm, pl.BlockSpec((1, d), lambda i, *_: (0, 0))],
            out_specs=pl.BlockSpec((tm_c, d), lambda i, *_: (i, 0)),
            scratch_shapes=[pltpu.VMEM((2, TOP_K, tm_c, d), F32), pltpu.SemaphoreType.DMA((2,))]),
        out_shape=jax.ShapeDtypeStruct((t, d), F32),
        compiler_params=_params(("arbitrary",)), name="moe_combine",
    )(pos, x, ys, g_final)


def _rot_cols(w):
    return jnp.concatenate([-w[..., ROPE_HALF:], w[..., :ROPE_HALF]], axis=-1)


def _pad_cols(w, n):
    return jnp.pad(w, ((0, 0), (0, n - w.shape[1])))


def _layer_weights(l, P):
    d = P["w_in"].shape[1]
    cq_n = P["mla_q_norm"].shape[1]
    ckv_n = P["mla_kv_norm"].shape[1]
    heads, nope = P["mla_w_uk"].shape[2], P["mla_w_uk"].shape[3]
    rope_n = P["mla_w_uq"].shape[3] - nope
    v_n = P["mla_w_uv"].shape[3]
    dh = P["diff_lambda_q1"].shape[1]
    n_groups, chunk = P["chunk_w_s"].shape[1], P["chunk_w_s"].shape[2]
    cw = P["chunk_ln_g"].shape[1]
    assert (cq_n, ckv_n, heads, rope_n, dh, n_groups, chunk, cw) == (256, 128, HEADS, 32, 32, 4, CHUNK, 256)
    dq_n, dk_n, dv_n = 256, 128, 128
    assert P["w_in"].shape[2] == cq_n + ckv_n + rope_n + dq_n + dk_n + dv_n + 2 * cw
    w_in = P["w_in"][l]
    o = 0
    parts = {}
    for name, n in (("cq", cq_n), ("ckv", ckv_n), ("kr", rope_n), ("dq", dq_n), ("dk", dk_n), ("dv", dv_n),
                    ("cu", cw), ("cv", cw)):
        parts[name] = w_in[:, o:o + n]
        o += n
    rot32 = lambda w: _rot_cols(w.reshape(d, -1, 2 * ROPE_HALF)).reshape(d, -1)
    win = jnp.concatenate([
        parts["cq"], parts["ckv"], _pad_cols(parts["kr"], LANES), _pad_cols(rot32(parts["kr"]), LANES),
        parts["dq"], rot32(parts["dq"]), parts["dk"], rot32(parts["dk"]), parts["dv"], parts["cu"], parts["cv"],
    ], axis=1).astype(BF16)

    uq = jnp.transpose(P["mla_w_uq"][l], (1, 0, 2))
    uk = jnp.transpose(P["mla_w_uk"][l], (1, 0, 2))
    wq_lat = _fold_q(uq[:, :, :nope], uk)
    uq_r = uq[:, :, nope:]
    pad_r = lambda w: jnp.transpose(jnp.pad(w, ((0, 0), (0, 0), (0, LANES - rope_n))), (1, 0, 2)).reshape(cq_n, -1)
    wq = jnp.concatenate([wq_lat, pad_r(uq_r), pad_r(_rot_cols(uq_r))], axis=1).astype(BF16)

    uv = jnp.transpose(P["mla_w_uv"][l], (1, 0, 2))
    wuv_pad = jnp.stack([jnp.pad(uv[h], ((0, 0), (h * v_n, (heads - 1 - h) * v_n))) for h in range(heads)])

    ws = P["chunk_w_s"][l]
    bs = P["chunk_b_s"][l]
    rep = lambda a: jnp.repeat(a, cw // n_groups, axis=-1)
    return dict(
        win=win, wq=wq, wuv_pad=wuv_pad.astype(BF16),
        gmix=P["norm_mix"][l][None], gq=P["mla_q_norm"][l][None], gkv=P["mla_kv_norm"][l][None],
        lng=P["chunk_ln_g"][l][None], lnb=P["chunk_ln_b"][l][None],
        ws=ws, bs_full=rep(jnp.transpose(bs)),
        ws_row0=rep(ws[:, 0, 0][None]), bs_row0=rep(bs[:, 0][None]),
        lam=jnp.stack([P["diff_lambda_q1"][l], P["diff_lambda_k1"][l], P["diff_lambda_q2"][l],
                       P["diff_lambda_k2"][l]]),
        subln=jnp.tile(P["diff_subln"][l], 2)[None],
        w_out=P["w_out"][l].astype(BF16), gffn=P["norm_ffn"][l][None],
        mla_scale=float((nope + rope_n) ** -0.5), diff_scale=float(dh ** -0.5),
    )


def _rope_tables(pos):
    inv = 1.0 / (ROPE_THETA ** (jnp.arange(ROPE_HALF, dtype=F32) / ROPE_HALF))
    ang = pos.astype(F32)[:, None] * inv[None, :]
    reps = LANES // ROPE_HALF
    return jnp.tile(jnp.cos(ang), (1, reps)), jnp.tile(jnp.sin(ang), (1, reps))


def kernel(x_prompt, x_sample, cache_mla_latent, cache_mla_krope, cache_diff_k, cache_diff_v, page_table, norm_mix, w_in, mla_q_norm, mla_w_uq, mla_kv_norm, mla_w_uk, mla_w_uv, diff_lambda_q1, diff_lambda_k1, diff_lambda_q2, diff_lambda_k2, diff_subln, chunk_ln_g, chunk_ln_b, chunk_w_s, chunk_b_s, w_out, norm_ffn, ffn_w_gate, ffn_w_up, ffn_w_down, moe_w_router, moe_w_gate, moe_w_up, moe_w_down, norm_final):
    P = dict(norm_mix=norm_mix, w_in=w_in, mla_q_norm=mla_q_norm, mla_w_uq=mla_w_uq, mla_kv_norm=mla_kv_norm,
             mla_w_uk=mla_w_uk, mla_w_uv=mla_w_uv, diff_lambda_q1=diff_lambda_q1, diff_lambda_k1=diff_lambda_k1,
             diff_lambda_q2=diff_lambda_q2, diff_lambda_k2=diff_lambda_k2, diff_subln=diff_subln,
             chunk_ln_g=chunk_ln_g, chunk_ln_b=chunk_ln_b, chunk_w_s=chunk_w_s, chunk_b_s=chunk_b_s,
             w_out=w_out, norm_ffn=norm_ffn)
    depth = w_in.shape[0]
    bp, s, d = x_prompt.shape
    bs, ts, _ = x_sample.shape
    assert bp == 1 and ts == 1
    n_pool, page = cache_mla_latent.shape[1], cache_mla_latent.shape[2]
    past_len = page_table.shape[1] * page
    assert past_len % CHUNK == 0
    n_experts = moe_w_router.shape[2]

    cos_p, sin_p = _rope_tables(jnp.arange(s))
    cos_s, sin_s = _rope_tables(jnp.full((bs,), past_len))
    c_krt = jnp.transpose(cache_mla_krope, (0, 1, 3, 2))
    c_dkt = jnp.transpose(cache_diff_k, (0, 1, 3, 4, 5, 2)).reshape(depth, n_pool, LANES, page)
    c_dvt = jnp.transpose(cache_diff_v, (0, 1, 3, 4, 2)).reshape(depth, n_pool, LANES, page)
    gfin = norm_final[None]

    xp = x_prompt.reshape(s, d)
    xs = x_sample.reshape(bs, d)
    st_p, st_s = [], []
    for l in range(depth):
        w = _layer_weights(l, P)
        lam_init = 0.8 - 0.6 * math.exp(-0.3 * l)
        last = l == depth - 1
        j = l // 2
        if l % 2 == 0:
            mix_w = (ffn_w_gate[j].astype(BF16), ffn_w_up[j].astype(BF16), ffn_w_down[j].astype(BF16))
        else:
            mix_w = (moe_w_gate[j].astype(BF16), moe_w_up[j].astype(BF16), moe_w_down[j].astype(BF16))
            wr_pad = _pad_cols(moe_w_router[j], LANES)

        def channel_mix(x, h, tm):
            if l % 2 == 0:
                return _ffn(x, h, *mix_w, gfin, final_norm=last, tm=tm, tf=mix_w[0].shape[1] // 2)
            gate, sel = _router(x, w["gffn"], wr_pad, n_experts, tm=tm)
            tf = mix_w[0].shape[2] // 2
            tm_e = 512
            if x.shape[0] * TOP_K < n_experts * tm_e:
                return _moe(x, h, gate, *mix_w, gfin, final_norm=last, tm=tm, tf=tf)
            return _moe_routed(x, sel, w["gffn"], *mix_w, gfin, final_norm=last, tm_e=tm_e, tf=tf, tm_c=256)

        qm, kc, qd, kd, vd, oc, ckv, kr, dk, dv, vst = _prep(xs, cos_s, sin_s, w, sample_mode=True, tm=128)
        om, od = _decode(l, page_table, jnp.transpose(qm, (1, 0, 2)), jnp.transpose(qd, (1, 0, 2)),
                         kc[:, None], kd[:, None], vd[:, None],
                         cache_mla_latent, c_krt, c_dkt, c_dvt, cp=32)
        o_mla, o_diff = _sample_epilogue(jnp.transpose(om, (1, 0, 2)), jnp.transpose(od, (1, 0, 2)), w, lam_init)
        xs, hs = _outproj(xs, o_mla, o_diff, oc, w["w_out"], w["gffn"], tm=128)
        xs = channel_mix(xs, hs, 128)
        st_s.append((ckv, kr, dk, dv, vst))

        qm, kc, qd, kd, _, oc, ckv, kr, dk, dv, vme, vde = _prep(xp, cos_p, sin_p, w, sample_mode=False, tm=256)
        o_mla = _flash(qm, kc, vme, (w["wuv_pad"],), mode="mla", lam_init=lam_init, tq=512, tk=1024, groups=8)
        o_diff = _flash(qd, kd, vde, (w["lam"], w["subln"]), mode="diff", lam_init=lam_init, tq=512, tk=2048,
                        groups=8)
        xp, hp = _outproj(xp, o_mla, o_diff, oc, w["w_out"], w["gffn"], tm=512)
        xp = channel_mix(xp, hp, 512)
        st_p.append((ckv, kr, dk, dv))

    stack = lambda sts, i, shape: jnp.stack([st[i] for st in sts], axis=0).reshape(shape)
    return (xp.reshape(bp, s, d), xs.reshape(bs, ts, d),
            stack(st_p, 0, (depth, bp, s, -1)), stack(st_p, 1, (depth, bp, s, -1)),
            stack(st_p, 2, (depth, bp, s) + cache_diff_k.shape[3:]),
            stack(st_p, 3, (depth, bp, s) + cache_diff_v.shape[3:]),
            stack(st_s, 0, (depth, bs, ts, -1)), stack(st_s, 1, (depth, bs, ts, -1)),
            stack(st_s, 2, (depth, bs, ts) + cache_diff_k.shape[3:]),
            stack(st_s, 3, (depth, bs, ts) + cache_diff_v.shape[3:]),
            stack(st_s, 4, (depth, bs, ts, -1)))
```
